```python
import math
import jax
import jax.numpy as jnp
from jax import lax
import numpy as np

D_MODEL = 1024
BATCH = 4
SEQ = 4096
DEPTH = 4

CHUNK = 64
Q_BLOCK = 128
D_FF = 2816
RMS_EPS = 1e-6
NEG_INF = -1e30

DIFF_HEADS = 4
DIFF_QK_DIM = 64
DIFF_V_DIM = 2 * DIFF_QK_DIM
SB_HEADS = 4
SB_HEAD_DIM = 128
DIFF_Q_COLS = DIFF_HEADS * 2 * DIFF_QK_DIM
DIFF_V_COLS = DIFF_HEADS * DIFF_V_DIM
SB_COLS = SB_HEADS * SB_HEAD_DIM
ATT_IN_DIM = 2 * DIFF_Q_COLS + DIFF_V_COLS + 3 * SB_COLS
ATT_MIX_DIM = DIFF_V_COLS + SB_COLS

GDN_HEADS = 8
GDN_HEAD_DIM = 128
GDN_CONV = 4
GDN_MIX_DIM = GDN_HEADS * GDN_HEAD_DIM
GDN_QKV_DIM = 3 * GDN_MIX_DIM
GDN_IN_DIM = GDN_QKV_DIM + GDN_MIX_DIM + 2 * GDN_HEADS

N_EVEN = (DEPTH + 1) // 2
N_ODD = DEPTH // 2

kernel_name = 'hybrid_streaming_diff_sb_gdn'


def rms_norm(x, w):
    x32 = x.astype(jnp.float32)
    y = x32 * lax.rsqrt(jnp.mean(x32 * x32, axis=-1, keepdims=True) + RMS_EPS)
    return (y * w.astype(jnp.float32)).astype(x.dtype)


def swiglu(h, w_gate, w_up, w_down):
    return (jax.nn.silu(h @ w_gate) * (h @ w_up)) @ w_down


def l2_normalize(x):
    return x * lax.rsqrt(jnp.sum(x * x, axis=-1, keepdims=True) + 1e-6)


def diff_attention(q, k, v, lam, slopes):
    b, s_len, h, _, _ = q.shape
    scale = DIFF_QK_DIM ** -0.5
    pos_k = jnp.arange(s_len)

    def block(i):
        start = i * Q_BLOCK
        qb = lax.dynamic_slice_in_dim(q, start, Q_BLOCK, axis=1)
        pos_q = start + jnp.arange(Q_BLOCK)
        dist = jnp.abs(pos_q[:, None] - pos_k[None, :]).astype(jnp.float32)
        allowed = (pos_k[None, :] // CHUNK) <= (pos_q[:, None] // CHUNK)
        bias = jnp.where(allowed, -slopes[:, None, None] * dist, NEG_INF)
        scores = jnp.einsum('bqhcd,bkhcd->bchqk', qb, k) * scale + bias
        p = jax.nn.softmax(scores, axis=-1)
        weights = p[:, 0] - lam * p[:, 1]
        return jnp.einsum('bhqk,bkhd->bqhd', weights, v)

    out = lax.map(block, jnp.arange(s_len // Q_BLOCK))
    return out.transpose(1, 0, 2, 3, 4).reshape(b, s_len, h, v.shape[-1])


def stick_breaking_attention(q, k, v):
    b, s_len, h, d = q.shape
    scale = SB_HEAD_DIM ** -0.5
    pos_k = jnp.arange(s_len)

    def block(i):
        start = i * Q_BLOCK
        qb = lax.dynamic_slice_in_dim(q, start, Q_BLOCK, axis=1)
        pos_q = start + jnp.arange(Q_BLOCK)
        earlier = pos_k[None, :] < pos_q[:, None]
        z = jnp.einsum('bqhd,bkhd->bhqk', qb, k) * scale
        log_beta = jax.nn.log_sigmoid(z)
        log_keep = jnp.where(earlier, jax.nn.log_sigmoid(-z), 0.0)
        log_after = lax.cumsum(log_keep, axis=3, reverse=True) - log_keep
        a = jnp.where(earlier, jnp.exp(log_beta + log_after), 0.0)
        return jnp.einsum('bhqk,bkhd->bqhd', a, v)

    out = lax.map(block, jnp.arange(s_len // Q_BLOCK))
    return out.transpose(1, 0, 2, 3, 4).reshape(b, s_len, h, d)


def causal_depthwise_conv(x, w):
    taps = w.shape[0]
    s_len = x.shape[1]
    xp = jnp.pad(x, ((0, 0), (taps - 1, 0), (0, 0)))
    y = xp[:, 0:s_len] * w[0]
    for i in range(1, taps):
        y = y + xp[:, i:i + s_len] * w[i]
    return y


def gated_delta_rule_chunked(q, k, v, g, beta):
    b, s_len, h, dk = q.shape
    dv = v.shape[-1]
    n_chunks = s_len // CHUNK

    def to_chunks(t):
        t = t.reshape((b, n_chunks, CHUNK, h) + t.shape[3:])
        return jnp.moveaxis(t, 3, 1)

    q = to_chunks(q * dk ** -0.5)
    k = to_chunks(k)
    v = to_chunks(v)
    g = to_chunks(g)
    beta = to_chunks(beta)
    gc = jnp.cumsum(g, axis=-1)
    incl = jnp.tril(jnp.ones((CHUNK, CHUNK), dtype=bool))
    strict = jnp.tril(jnp.ones((CHUNK, CHUNK), dtype=bool), -1)
    decay = jnp.exp(jnp.where(incl, gc[..., :, None] - gc[..., None, :], -jnp.inf))
    kb = k * beta[..., None]
    vb = v * beta[..., None]
    l_mat = jnp.where(strict, jnp.einsum('bhncd,bhnsd->bhncs', kb, k) * decay, 0.0)
    eye = jnp.eye(CHUNK, dtype=q.dtype)
    t_mat = lax.linalg.triangular_solve(l_mat + eye, jnp.broadcast_to(eye, l_mat.shape),
                                        left_side=True, lower=True, unit_diagonal=True)
    u = jnp.einsum('bhncs,bhnsd->bhncd', t_mat, vb)
    w = jnp.einsum('bhncs,bhnsd->bhncd', t_mat, kb * jnp.exp(gc)[..., None])
    a_qk = jnp.where(incl, jnp.einsum('bhncd,bhnsd->bhncs', q, k) * decay, 0.0)
    q_dec = q * jnp.exp(gc)[..., None]
    k_dec = k * jnp.exp(gc[..., -1:] - gc)[..., None]
    chunk_decay = jnp.exp(gc[..., -1])
    xs = (jnp.moveaxis(q_dec, 2, 0), jnp.moveaxis(k_dec, 2, 0), jnp.moveaxis(u, 2, 0),
          jnp.moveaxis(w, 2, 0), jnp.moveaxis(a_qk, 2, 0), jnp.moveaxis(chunk_decay, 2, 0))

    def step(state, inp):
        qd, kd, un, wn, aqk, cd = inp
        v_new = un - jnp.einsum('bhcd,bhde->bhce', wn, state)
        o = jnp.einsum('bhcd,bhde->bhce', qd, state) + jnp.einsum('bhcs,bhse->bhce', aqk, v_new)
        state = state * cd[..., None, None] + jnp.einsum('bhcd,bhce->bhde', kd, v_new)
        return state, o

    state0 = jnp.zeros((b, h, dk, dv), dtype=q.dtype)
    _, out = lax.scan(step, state0, xs)
    return out.transpose(1, 0, 3, 2, 4).reshape(b, s_len, h, dv)


def even_mixer(h, w_in, diff_lambda, diff_subln, w_out, layer_idx):
    b, s_len, _ = h.shape
    proj = (h @ w_in).astype(jnp.float32)
    splits = [DIFF_Q_COLS, 2 * DIFF_Q_COLS, 2 * DIFF_Q_COLS + DIFF_V_COLS,
              2 * DIFF_Q_COLS + DIFF_V_COLS + SB_COLS, 2 * DIFF_Q_COLS + DIFF_V_COLS + 2 * SB_COLS]
    qa, ka, va, qs, ks, vs = jnp.split(proj, splits, axis=-1)
    qa = qa.reshape(b, s_len, DIFF_HEADS, 2, DIFF_QK_DIM)
    ka = ka.reshape(b, s_len, DIFF_HEADS, 2, DIFF_QK_DIM)
    va = va.reshape(b, s_len, DIFF_HEADS, DIFF_V_DIM)
    qs = qs.reshape(b, s_len, SB_HEADS, SB_HEAD_DIM)
    ks = ks.reshape(b, s_len, SB_HEADS, SB_HEAD_DIM)
    vs = vs.reshape(b, s_len, SB_HEADS, SB_HEAD_DIM)
    lambda_init = 0.8 - 0.6 * math.exp(-0.3 * layer_idx)
    lp = diff_lambda.astype(jnp.float32)
    lam = jnp.exp(jnp.sum(lp[0] * lp[1])) - jnp.exp(jnp.sum(lp[2] * lp[3])) + lambda_init
    slopes = 2.0 ** (-8.0 * jnp.arange(1, DIFF_HEADS + 1, dtype=jnp.float32) / DIFF_HEADS)
    oa = diff_attention(qa, ka, va, lam, slopes)
    oa = rms_norm(oa, diff_subln) * (1.0 - lambda_init)
    osb = stick_breaking_attention(qs, ks, vs)
    o = jnp.concatenate([oa.reshape(b, s_len, DIFF_V_COLS), osb.reshape(b, s_len, SB_COLS)], axis=-1)
    return o.astype(h.dtype) @ w_out


def odd_mixer(h, w_in, conv_w, a_log, dt_bias, norm_w, w_out):
    b, s_len, _ = h.shape
    proj = h @ w_in
    qkv, gate, b_raw, a_raw = jnp.split(
        proj, [GDN_QKV_DIM, GDN_QKV_DIM + GDN_MIX_DIM, GDN_QKV_DIM + GDN_MIX_DIM + GDN_HEADS], axis=-1)
    qkv = jax.nn.silu(causal_depthwise_conv(qkv, conv_w)).astype(jnp.float32)
    q, k, v = jnp.split(qkv, 3, axis=-1)
    q = l2_normalize(q.reshape(b, s_len, GDN_HEADS, GDN_HEAD_DIM))
    k = l2_normalize(k.reshape(b, s_len, GDN_HEADS, GDN_HEAD_DIM))
    v = v.reshape(b, s_len, GDN_HEADS, GDN_HEAD_DIM)
    beta = jax.nn.sigmoid(b_raw.astype(jnp.float32))
    g = -jnp.exp(a_log.astype(jnp.float32)) * jax.nn.softplus(
        a_raw.astype(jnp.float32) + dt_bias.astype(jnp.float32))
    o = gated_delta_rule_chunked(q, k, v, g, beta)
    o = rms_norm(o, norm_w) * jax.nn.silu(gate.astype(jnp.float32).reshape(b, s_len, GDN_HEADS, GDN_HEAD_DIM))
    return o.reshape(b, s_len, GDN_MIX_DIM).astype(h.dtype) @ w_out


def setup_inputs(seed: int = 0) -> dict:
    key = jax.random.key(seed)
    ks = jax.random.split(key, 24)
    f32 = jnp.float32

    def dense(k, shape, fan_in):
        return jax.random.normal(k, shape, f32) * fan_in ** -0.5

    def gain(k, shape):
        return 1.0 + 0.02 * jax.random.normal(k, shape, f32)

    x = jax.random.normal(ks[0], (BATCH, SEQ, D_MODEL), f32)
    ffn1_norm = gain(ks[1], (DEPTH, D_MODEL))
    ffn1_w_gate = dense(ks[2], (DEPTH, D_MODEL, D_FF), D_MODEL)
    ffn1_w_up = dense(ks[3], (DEPTH, D_MODEL, D_FF), D_MODEL)
    ffn1_w_down = dense(ks[4], (DEPTH, D_FF, D_MODEL), D_FF)
    mix_norm = gain(ks[5], (DEPTH, D_MODEL))
    att_w_in = dense(ks[6], (N_EVEN, D_MODEL, ATT_IN_DIM), D_MODEL)
    diff_lambda = 0.1 * jax.random.normal(ks[7], (N_EVEN, 4, DIFF_QK_DIM), f32)
    diff_subln = gain(ks[8], (N_EVEN, DIFF_V_DIM))
    att_w_out = dense(ks[9], (N_EVEN, ATT_MIX_DIM, D_MODEL), ATT_MIX_DIM)
    gdn_w_in = dense(ks[10], (N_ODD, D_MODEL, GDN_IN_DIM), D_MODEL)
    gdn_conv_w = dense(ks[11], (N_ODD, GDN_CONV, GDN_QKV_DIM), GDN_CONV)
    gdn_a_log = jnp.log(jax.random.uniform(ks[12], (N_ODD, GDN_HEADS), f32, 1.0, 16.0))
    dt = jnp.exp(jax.random.uniform(ks[13], (N_ODD, GDN_HEADS), f32, math.log(1e-3), math.log(1e-1)))
    gdn_dt_bias = dt + jnp.log(-jnp.expm1(-dt))
    gdn_norm = gain(ks[14], (N_ODD, GDN_HEAD_DIM))
    gdn_w_out = dense(ks[15], (N_ODD, GDN_MIX_DIM, D_MODEL), GDN_MIX_DIM)
    ffn2_norm = gain(ks[16], (DEPTH, D_MODEL))
    ffn2_w_gate = dense(ks[17], (DEPTH, D_MODEL, D_FF), D_MODEL)
    ffn2_w_up = dense(ks[18], (DEPTH, D_MODEL, D_FF), D_MODEL)
    ffn2_w_down = dense(ks[19], (DEPTH, D_FF, D_MODEL), D_FF)
    final_norm = gain(ks[20], (D_MODEL,))
    return {'x': x, 'ffn1_norm': ffn1_norm, 'ffn1_w_gate': ffn1_w_gate, 'ffn1_w_up': ffn1_w_up,
            'ffn1_w_down': ffn1_w_down, 'mix_norm': mix_norm, 'att_w_in': att_w_in,
            'diff_lambda': diff_lambda, 'diff_subln': diff_subln, 'att_w_out': att_w_out,
            'gdn_w_in': gdn_w_in, 'gdn_conv_w': gdn_conv_w, 'gdn_a_log': gdn_a_log,
            'gdn_dt_bias': gdn_dt_bias, 'gdn_norm': gdn_norm, 'gdn_w_out': gdn_w_out,
            'ffn2_norm': ffn2_norm, 'ffn2_w_gate': ffn2_w_gate, 'ffn2_w_up': ffn2_w_up,
            'ffn2_w_down': ffn2_w_down, 'final_norm': final_norm}


def reference(x, ffn1_norm, ffn1_w_gate, ffn1_w_up, ffn1_w_down, mix_norm, att_w_in,
              diff_lambda, diff_subln, att_w_out, gdn_w_in, gdn_conv_w, gdn_a_log,
              gdn_dt_bias, gdn_norm, gdn_w_out, ffn2_norm, ffn2_w_gate, ffn2_w_up,
              ffn2_w_down, final_norm):
    for layer in range(DEPTH):
        x = x + 0.5 * swiglu(rms_norm(x, ffn1_norm[layer]), ffn1_w_gate[layer],
                             ffn1_w_up[layer], ffn1_w_down[layer])
        h = rms_norm(x, mix_norm[layer])
        if layer % 2 == 0:
            e = layer // 2
            x = x + even_mixer(h, att_w_in[e], diff_lambda[e], diff_subln[e], att_w_out[e], layer)
        else:
            o = layer // 2
            x = x + odd_mixer(h, gdn_w_in[o], gdn_conv_w[o], gdn_a_log[o], gdn_dt_bias[o],
                              gdn_norm[o], gdn_w_out[o])
        x = x + 0.5 * swiglu(rms_norm(x, ffn2_norm[layer]), ffn2_w_gate[layer],
                             ffn2_w_up[layer], ffn2_w_down[layer])
    return rms_norm(x, final_norm)
```

```python
import functools
import math

import jax
import jax.numpy as jnp
from jax import lax
from jax.experimental import pallas as pl
from jax.experimental.pallas import tpu as pltpu

F32 = jnp.float32
BF16 = jnp.bfloat16

D_FF = 2816
RMS_EPS = 1e-6
L2_EPS = 1e-6
NEG_INF = -1e30

MASK_CHUNK_LOG2 = 6
DIFF_HEADS = 4
DIFF_QK_DIM = 64
SB_HEADS = 4
HEAD_COLS = 128
GDN_HEADS = 8
GDN_CONV = 4
GDN_CHUNK = 128

V7X_VMEM_LIMIT_BYTES = 56 * 1024 * 1024
CONV_HALO = 16


def _cparams(*semantics):
    return pltpu.CompilerParams(dimension_semantics=semantics, vmem_limit_bytes=V7X_VMEM_LIMIT_BYTES)


def _resident(shape):
    zeros = (0,) * len(shape)
    return pl.BlockSpec(shape, lambda *_: zeros, pipeline_mode=pl.Buffered(1))


def _rms(x, w):
    ms = jnp.mean(x * x, axis=-1, keepdims=True)
    return x * lax.rsqrt(ms + RMS_EPS) * w


def _silu(x):
    return x * jax.nn.sigmoid(x)


def _softplus(x):
    return jnp.maximum(x, 0.0) + jnp.log(1.0 + jnp.exp(-jnp.abs(x)))


def _dot(a, b):
    return jnp.dot(a, b, preferred_element_type=F32)


def _dot_nt(a, b):
    return lax.dot_general(a, b, (((1,), (1,)), ((), ())), preferred_element_type=F32)


def _ffn_body(x_ref, nw_ref, wg_ref, wu_ref, wd_ref, *rest, ff_chunk, final):
    if final:
        fw_ref, o_ref, a_ref = rest
    else:
        o_ref, a_ref = rest
    x = x_ref[...]
    h = _rms(x, nw_ref[...]).astype(BF16)
    for c in range(0, D_FF, ff_chunk):
        g = _dot(h, wg_ref[:, c:c + ff_chunk])
        u = _dot(h, wu_ref[:, c:c + ff_chunk])
        a_ref[:, c:c + ff_chunk] = (_silu(g) * u).astype(BF16)
    y = x + 0.5 * _dot(a_ref[...], wd_ref[...])
    if final:
        y = _rms(y, fw_ref[...])
    o_ref[...] = y


def _ffn(x, nw, wg, wu, wd, final_w=None, *, tm=512, ff_chunk=256):
    m, d = x.shape
    final = final_w is not None
    row = pl.BlockSpec((tm, d), lambda i: (i, 0))
    in_specs = [row, _resident((1, d)), _resident((d, D_FF)), _resident((d, D_FF)), _resident((D_FF, d))]
    args = [x, nw.reshape(1, d), wg.astype(BF16), wu.astype(BF16), wd.astype(BF16)]
    if final:
        in_specs.append(_resident((1, d)))
        args.append(final_w.reshape(1, d))
    return pl.pallas_call(
        functools.partial(_ffn_body, ff_chunk=ff_chunk, final=final),
        grid=(m // tm,),
        in_specs=in_specs,
        out_specs=row,
        out_shape=jax.ShapeDtypeStruct((m, d), F32),
        scratch_shapes=[pltpu.VMEM((tm, D_FF), BF16)],
        compiler_params=_cparams("parallel"),
        name="ffn",
    )(*args)


def _proj_out_body(*refs, n_lhs):
    x_ref = refs[0]
    o_ref = refs[-1]
    y = x_ref[...]
    for a_ref, w_ref in zip(refs[1:1 + n_lhs], refs[1 + n_lhs:1 + 2 * n_lhs]):
        y = y + _dot(a_ref[...], w_ref[...])
    o_ref[...] = y


def _proj_out(x, lhs, ws, *, tm=512):
    m, d = x.shape
    row = pl.BlockSpec((tm, d), lambda i: (i, 0))
    in_specs = [row]
    in_specs += [pl.BlockSpec((tm, a.shape[1]), lambda i: (i, 0)) for a in lhs]
    in_specs += [_resident(w.shape) for w in ws]
    return pl.pallas_call(
        functools.partial(_proj_out_body, n_lhs=len(lhs)),
        grid=(m // tm,),
        in_specs=in_specs,
        out_specs=row,
        out_shape=jax.ShapeDtypeStruct((m, d), F32),
        compiler_params=_cparams("parallel"),
        name="proj_out",
    )(x, *lhs, *[w.astype(BF16) for w in ws])


def _att_in_body(x_ref, nw_ref, w_ref, sc_ref, o_ref, *, n_chunk):
    h = _rms(x_ref[...], nw_ref[...]).astype(BF16)
    n = w_ref.shape[1]
    for c in range(0, n, n_chunk):
        p = _dot(h, w_ref[:, c:c + n_chunk])
        o_ref[:, c:c + n_chunk] = (p * sc_ref[:, c:c + n_chunk]).astype(BF16)


def _att_in(x, nw, w, col_scale, *, tm=512, n_chunk=512):
    m, d = x.shape
    n = w.shape[1]
    return pl.pallas_call(
        functools.partial(_att_in_body, n_chunk=n_chunk),
        grid=(m // tm,),
        in_specs=[pl.BlockSpec((tm, d), lambda i: (i, 0)), _resident((1, d)), _resident((d, n)),
                  _resident((1, n))],
        out_specs=pl.BlockSpec((tm, n), lambda i: (i, 0)),
        out_shape=jax.ShapeDtypeStruct((m, n), BF16),
        compiler_params=_cparams("parallel"),
        name="att_in",
    )(x, nw.reshape(1, d), w.astype(BF16), col_scale)


def _diff_body(slope_ref, lam_ref, q_ref, k_ref, v_ref, sw_ref, o_ref, *, tq, out_scale):
    h = pl.program_id(1)
    i = pl.program_id(2)
    slope = slope_ref[h]
    lam = lam_ref[0]
    q = q_ref[...]
    lane = lax.broadcasted_iota(jnp.int32, q.shape, 1)
    zero = jnp.zeros_like(q)
    q_comp = (jnp.where(lane < DIFF_QK_DIM, q, zero), jnp.where(lane >= DIFF_QK_DIM, q, zero))
    ri = lax.broadcasted_iota(jnp.int32, (tq, tq), 0)
    cj = lax.broadcasted_iota(jnp.int32, (tq, tq), 1)
    rel = (ri - cj).astype(F32)
    chunk_gap = (cj >> MASK_CHUNK_LOG2) - (ri >> MASK_CHUNK_LOG2)

    def body(j, carry):
        off = (i - j) * tq
        start = pl.multiple_of(j * tq, tq)
        kb = k_ref[pl.ds(start, tq), :]
        vb = v_ref[pl.ds(start, tq), :]
        bias = -slope * jnp.abs(rel + off.astype(F32))
        bias = jnp.where(chunk_gap <= (off >> MASK_CHUNK_LOG2), bias, NEG_INF)
        out = []
        for c in range(2):
            m, l, acc = carry[3 * c:3 * c + 3]
            s = _dot_nt(q_comp[c], kb) + bias
            m_new = jnp.maximum(m, jnp.max(s, axis=-1, keepdims=True))
            alpha = jnp.exp(m - m_new)
            p = jnp.exp(s - m_new)
            l = alpha * l + jnp.sum(p, axis=-1, keepdims=True)
            acc = alpha * acc + _dot(p.astype(BF16), vb)
            out += [m_new, l, acc]
        return tuple(out)

    init = (jnp.full((tq, 1), NEG_INF, F32), jnp.zeros((tq, 1), F32), jnp.zeros((tq, HEAD_COLS), F32)) * 2
    m0, l0, a0, m1, l1, a1 = lax.fori_loop(0, i + 1, body, init)
    o = a0 / l0 - lam * (a1 / l1)
    o_ref[...] = (_rms(o, sw_ref[...]) * out_scale).astype(BF16)


def _diff_attn(proj, slopes, lam, subln, *, batch, seq, out_scale, tq=256):
    smem = pl.BlockSpec(memory_space=pltpu.SMEM)
    kv = lambda blk: pl.BlockSpec((None, seq, HEAD_COLS), lambda b, h, i: (b, 0, blk + h))
    return pl.pallas_call(
        functools.partial(_diff_body, tq=tq, out_scale=out_scale),
        grid=(batch, DIFF_HEADS, seq // tq),
        in_specs=[smem, smem,
                  pl.BlockSpec((None, tq, HEAD_COLS), lambda b, h, i: (b, i, h)),
                  kv(DIFF_HEADS), kv(2 * DIFF_HEADS),
                  pl.BlockSpec((1, HEAD_COLS), lambda b, h, i: (0, 0))],
        out_specs=pl.BlockSpec((None, tq, HEAD_COLS), lambda b, h, i: (b, i, h)),
        out_shape=jax.ShapeDtypeStruct((batch, seq, DIFF_HEADS * HEAD_COLS), BF16),
        compiler_params=_cparams("parallel", "parallel", "arbitrary"),
        name="diff_attn",
    )(slopes, lam, proj, proj, proj, subln.reshape(1, HEAD_COLS))


def _sb_body(q_ref, k_ref, v_ref, o_ref, *, tq):
    i = pl.program_id(2)
    q = q_ref[...]
    ri = lax.broadcasted_iota(jnp.int32, (tq, tq), 0)
    cj = lax.broadcasted_iota(jnp.int32, (tq, tq), 1)
    key_minus_query = cj - ri
    later = (ri > cj).astype(BF16)

    def body(t, carry):
        tail, acc = carry
        j = i - t
        off = (i - j) * tq
        start = pl.multiple_of(j * tq, tq)
        kb = k_ref[pl.ds(start, tq), :]
        vb = v_ref[pl.ds(start, tq), :]
        z = _dot_nt(q, kb)
        sp = jnp.log(1.0 + jnp.exp(-jnp.abs(z)))
        log_beta = jnp.minimum(z, 0.0) - sp
        earlier = key_minus_query < off
        log_keep = jnp.where(earlier, -jnp.maximum(z, 0.0) - sp, 0.0)
        hi = log_keep.astype(BF16)
        lo = (log_keep - hi.astype(F32)).astype(BF16)
        after = _dot(hi, later) + _dot(lo, later)
        a = jnp.where(earlier, jnp.exp(log_beta + after + tail), 0.0)
        acc = acc + _dot(a.astype(BF16), vb)
        tail = tail + jnp.sum(log_keep, axis=-1, keepdims=True)
        return tail, acc

    init = (jnp.zeros((tq, 1), F32), jnp.zeros((tq, HEAD_COLS), F32))
    _, acc = lax.fori_loop(0, i + 1, body, init)
    o_ref[...] = acc.astype(BF16)


def _sb_attn(proj, *, batch, seq, tq=256):
    base = 3 * DIFF_HEADS
    kv = lambda blk: pl.BlockSpec((None, seq, HEAD_COLS), lambda b, h, i: (b, 0, blk + h))
    return pl.pallas_call(
        functools.partial(_sb_body, tq=tq),
        grid=(batch, SB_HEADS, seq // tq),
        in_specs=[pl.BlockSpec((None, tq, HEAD_COLS), lambda b, h, i: (b, i, base + h)),
                  kv(base + SB_HEADS), kv(base + 2 * SB_HEADS)],
        out_specs=pl.BlockSpec((None, tq, HEAD_COLS), lambda b, h, i: (b, i, h)),
        out_shape=jax.ShapeDtypeStruct((batch, seq, SB_HEADS * HEAD_COLS), BF16),
        compiler_params=_cparams("parallel", "parallel", "arbitrary"),
        name="sb_attn",
    )(proj, proj, proj)


def _gdn_in_body(x_ref, xp_ref, nw_ref, wqkv_ref, wg_ref, wba_ref, cw_ref, alog_ref, dt_ref,
                 q_ref, k_ref, v_ref, gate_ref, bg_ref, p_scr, *, tm, seq, n_chunk):
    i = pl.program_id(0)
    nw = nw_ref[...]
    h = _rms(x_ref[...], nw).astype(BF16)
    seq_start = (i * tm) % seq == 0
    hp = jnp.where(seq_start, 0.0, _rms(xp_ref[...], nw)).astype(BF16)
    mix = GDN_HEADS * HEAD_COLS
    outs = (q_ref, k_ref, v_ref)
    q_scale = HEAD_COLS ** -0.5
    for c in range(0, 3 * mix, n_chunk):
        p_scr[0:CONV_HALO, :] = _dot(hp, wqkv_ref[:, c:c + n_chunk])
        p_scr[CONV_HALO:CONV_HALO + tm, :] = _dot(h, wqkv_ref[:, c:c + n_chunk])
        for s in range(0, n_chunk, HEAD_COLS):
            col = c + s
            y = None
            for tap in range(GDN_CONV):
                shift = CONV_HALO - (GDN_CONV - 1) + tap
                term = p_scr[shift:shift + tm, s:s + HEAD_COLS] * cw_ref[tap:tap + 1, col:col + HEAD_COLS]
                y = term if y is None else y + term
            y = _silu(y)
            which, local = divmod(col, mix)
            if which < 2:
                y = y * lax.rsqrt(jnp.sum(y * y, axis=-1, keepdims=True) + L2_EPS)
            if which == 0:
                y = y * q_scale
            outs[which][:, local:local + HEAD_COLS] = y.astype(BF16)
    for c in range(0, mix, n_chunk):
        gate_ref[:, c:c + n_chunk] = _dot(h, wg_ref[:, c:c + n_chunk]).astype(BF16)
    ba = _dot(h, wba_ref[...])
    lane = lax.broadcasted_iota(jnp.int32, ba.shape, 1)
    beta = jax.nn.sigmoid(ba)
    g = -jnp.exp(alog_ref[...]) * _softplus(ba + dt_ref[...])
    bg_ref[...] = jnp.where(lane < GDN_HEADS, beta, jnp.where(lane < 2 * GDN_HEADS, g, 0.0))


def _gdn_in(x, nw, w_in, conv_w, a_log, dt_bias, *, seq, tm=512, n_chunk=512):
    m, d = x.shape
    mix = GDN_HEADS * HEAD_COLS
    w_qkv = w_in[:, :3 * mix].astype(BF16)
    w_gate = w_in[:, 3 * mix:4 * mix].astype(BF16)
    pad = HEAD_COLS - 2 * GDN_HEADS
    w_ba = jnp.pad(w_in[:, 4 * mix:], ((0, 0), (0, pad))).astype(BF16)
    lanes = lambda v: jnp.pad(v.astype(F32), (GDN_HEADS, pad)).reshape(1, HEAD_COLS)
    row = lambda n: pl.BlockSpec((tm, n), lambda i: (i, 0))
    halo_blocks = tm // CONV_HALO
    act = jax.ShapeDtypeStruct((m, mix), BF16)
    return pl.pallas_call(
        functools.partial(_gdn_in_body, tm=tm, seq=seq, n_chunk=n_chunk),
        grid=(m // tm,),
        in_specs=[row(d),
                  pl.BlockSpec((CONV_HALO, d), lambda i: (jnp.maximum(i * halo_blocks - 1, 0), 0)),
                  _resident((1, d)), _resident((d, 3 * mix)), _resident((d, mix)), _resident((d, HEAD_COLS)),
                  _resident((GDN_CONV, 3 * mix)), _resident((1, HEAD_COLS)), _resident((1, HEAD_COLS))],
        out_specs=[row(mix), row(mix), row(mix), row(mix), row(HEAD_COLS)],
        out_shape=[act, act, act, act, jax.ShapeDtypeStruct((m, HEAD_COLS), F32)],
        scratch_shapes=[pltpu.VMEM((CONV_HALO + tm, n_chunk), F32)],
        compiler_params=_cparams("parallel"),
        name="gdn_in",
    )(x, x, nw.reshape(1, d), w_qkv, w_gate, w_ba, conv_w.astype(F32), lanes(a_log), lanes(dt_bias))


def _split3(x):
    b1 = x.astype(BF16)
    r1 = x - b1.astype(F32)
    b2 = r1.astype(BF16)
    b3 = (r1 - b2.astype(F32)).astype(BF16)
    return b1, b2, b3


def _hi_lo(x):
    hi = x.astype(BF16)
    return hi, (x - hi.astype(F32)).astype(BF16)


def _dot_split(a_parts, b_parts):
    ah, al = a_parts
    bh, bl = b_parts
    c = bh.shape[0]
    rhs = jnp.concatenate([jnp.concatenate([bh, bl], axis=1),
                           jnp.concatenate([bh, jnp.zeros_like(bh)], axis=1)], axis=0)
    r = _dot(jnp.concatenate([ah, al], axis=1), rhs)
    return r[:, :c] + r[:, c:]


def _unit_lower_inverse(low, eye):
    c = low.shape[0]
    low_parts = _hi_lo(low)
    pw = _dot_split(low_parts, low_parts)
    t_inv = eye - low
    for _ in range(int(math.log2(c)) - 2):
        pw_parts = _hi_lo(pw)
        t_parts = _hi_lo(t_inv)
        stacked = tuple(jnp.concatenate([t, p], axis=0) for t, p in zip(t_parts, pw_parts))
        both = _dot_split(stacked, pw_parts)
        t_inv = t_inv + both[:c]
        pw = both[c:]
    return t_inv + _dot_split(_hi_lo(t_inv), _hi_lo(pw))


def _gdn_body(q_ref, k_ref, v_ref, gate_ref, bg_ref, nw_ref, o_ref, s_scr):
    c = GDN_CHUNK

    @pl.when(pl.program_id(1) == 0)
    def _():
        s_scr[...] = jnp.zeros_like(s_scr)

    row = lax.broadcasted_iota(jnp.int32, (c, c), 0)
    col = lax.broadcasted_iota(jnp.int32, (c, c), 1)
    incl = col <= row
    strict = col < row
    eye = (col == row).astype(F32)
    tri = incl.astype(BF16)
    bg = bg_ref[...]
    gc_all = sum(_dot(tri, part) for part in _split3(bg))
    nw = nw_ref[...]
    for h in range(GDN_HEADS):
        sl = slice(h * HEAD_COLS, (h + 1) * HEAD_COLS)
        q = q_ref[:, sl].astype(F32)
        k = k_ref[:, sl].astype(F32)
        v = v_ref[:, sl].astype(F32)
        beta = bg[:, h:h + 1]
        gcb = jnp.broadcast_to(gc_all[:, GDN_HEADS + h:GDN_HEADS + h + 1], (c, c))
        decay = jnp.exp(jnp.where(incl, gcb - gcb.T, NEG_INF))
        eg = jnp.exp(gcb)
        g_last = gcb[c - 1:c, :]
        kb = k * beta
        vb = v * beta
        kbf = k.astype(BF16)
        low = jnp.where(strict, _dot_nt(kb.astype(BF16), kbf) * decay, 0.0)
        aqk = jnp.where(incl, _dot_nt(q.astype(BF16), kbf) * decay, 0.0)
        tb = _unit_lower_inverse(low, eye).astype(BF16)
        u = _dot(tb, vb.astype(BF16))
        w = _dot(tb, (kb * eg).astype(BF16))
        state = s_scr[h]
        sb = state.astype(BF16)
        v_new = u - _dot(w.astype(BF16), sb)
        vnb = v_new.astype(BF16)
        o = _dot((q * eg).astype(BF16), sb) + _dot(aqk.astype(BF16), vnb)
        k_dec = k * jnp.exp(g_last - gcb)
        s_scr[h] = state * jnp.exp(g_last) + _dot(k_dec.T.astype(BF16), vnb)
        o_ref[:, sl] = (_rms(o, nw) * _silu(gate_ref[:, sl].astype(F32))).astype(BF16)


def _gdn(q, k, v, gate, bg, norm_w, *, batch, seq):
    mix = GDN_HEADS * HEAD_COLS
    c = GDN_CHUNK
    per_seq = seq // c
    row = lambda n: pl.BlockSpec((c, n), lambda b, t: (b * per_seq + t, 0))
    return pl.pallas_call(
        _gdn_body,
        grid=(batch, per_seq),
        in_specs=[row(mix), row(mix), row(mix), row(mix), row(HEAD_COLS),
                  pl.BlockSpec((1, HEAD_COLS), lambda b, t: (0, 0))],
        out_specs=row(mix),
        out_shape=jax.ShapeDtypeStruct((batch * seq, mix), BF16),
        scratch_shapes=[pltpu.VMEM((GDN_HEADS, HEAD_COLS, HEAD_COLS), F32)],
        compiler_params=_cparams("parallel", "arbitrary"),
        name="gdn",
    )(q, k, v, gate, bg, norm_w.reshape(1, HEAD_COLS))


def _even_mixer(x, norm_w, w_in, diff_lambda, diff_subln, w_out, layer_idx, *, batch, seq):
    n_in = w_in.shape[1]
    q_cols = DIFF_HEADS * HEAD_COLS
    sb_q0 = 3 * q_cols
    col = jnp.arange(n_in)
    col_scale = jnp.where(col < q_cols, DIFF_QK_DIM ** -0.5,
                          jnp.where((col >= sb_q0) & (col < sb_q0 + SB_HEADS * HEAD_COLS), HEAD_COLS ** -0.5, 1.0))
    proj = _att_in(x, norm_w, w_in, col_scale.astype(F32).reshape(1, n_in)).reshape(batch, seq, n_in)
    lambda_init = 0.8 - 0.6 * math.exp(-0.3 * layer_idx)
    lp = diff_lambda.astype(F32)
    lam = jnp.exp(jnp.sum(lp[0] * lp[1])) - jnp.exp(jnp.sum(lp[2] * lp[3])) + lambda_init
    slopes = 2.0 ** (-8.0 * jnp.arange(1, DIFF_HEADS + 1, dtype=F32) / DIFF_HEADS)
    oa = _diff_attn(proj, slopes, lam.reshape(1), diff_subln.astype(F32), batch=batch, seq=seq,
                    out_scale=1.0 - lambda_init)
    osb = _sb_attn(proj, batch=batch, seq=seq)
    m = batch * seq
    return _proj_out(x, [oa.reshape(m, -1), osb.reshape(m, -1)], [w_out[:q_cols], w_out[q_cols:]])


def _odd_mixer(x, norm_w, w_in, conv_w, a_log, dt_bias, gdn_norm, w_out, *, batch, seq):
    q, k, v, gate, bg = _gdn_in(x, norm_w, w_in, conv_w, a_log, dt_bias, seq=seq)
    o = _gdn(q, k, v, gate, bg, gdn_norm.astype(F32), batch=batch, seq=seq)
    return _proj_out(x, [o], [w_out])


def kernel(x, ffn1_norm, ffn1_w_gate, ffn1_w_up, ffn1_w_down, mix_norm, att_w_in, diff_lambda, diff_subln, att_w_out, gdn_w_in, gdn_conv_w, gdn_a_log, gdn_dt_bias, gdn_norm, gdn_w_out, ffn2_norm, ffn2_w_gate, ffn2_w_up, ffn2_w_down, final_norm):
    batch, seq, d = x.shape
    depth = ffn1_norm.shape[0]
    xf = x.reshape(batch * seq, d)
    for layer in range(depth):
        xf = _ffn(xf, ffn1_norm[layer], ffn1_w_gate[layer], ffn1_w_up[layer], ffn1_w_down[layer])
        if layer % 2 == 0:
            e = layer // 2
            xf = _even_mixer(xf, mix_norm[layer], att_w_in[e], diff_lambda[e], diff_subln[e], att_w_out[e], layer,
                             batch=batch, seq=seq)
        else:
            o = layer // 2
            xf = _odd_mixer(xf, mix_norm[layer], gdn_w_in[o], gdn_conv_w[o], gdn_a_log[o], gdn_dt_bias[o],
                            gdn_norm[o], gdn_w_out[o], batch=batch, seq=seq)
        last = layer == depth - 1
        xf = _ffn(xf, ffn2_norm[layer], ffn2_w_gate[layer], ffn2_w_up[layer], ffn2_w_down[layer],
                  final_w=final_norm if last else None)
    return xf.reshape(batch, seq, d)
```

```python
import functools
import math

import jax
import jax.numpy as jnp
from jax import lax
from jax.experimental import pallas as pl
from jax.experimental.pallas import tpu as pltpu

F32 = jnp.float32
BF16 = jnp.bfloat16

D_FF = 2816
RMS_EPS = 1e-6
L2_EPS = 1e-6
NEG_INF = -1e30

MASK_CHUNK_LOG2 = 6
DIFF_HEADS = 4
DIFF_QK_DIM = 64
SB_HEADS = 4
HEAD_COLS = 128
GDN_HEADS = 8
GDN_CONV = 4
GDN_CHUNK = 128

V7X_VMEM_LIMIT_BYTES = 56 * 1024 * 1024
CONV_HALO = 16


def _cparams(*semantics):
    return pltpu.CompilerParams(dimension_semantics=semantics, vmem_limit_bytes=V7X_VMEM_LIMIT_BYTES)


def _resident(shape):
    zeros = (0,) * len(shape)
    return pl.BlockSpec(shape, lambda *_: zeros, pipeline_mode=pl.Buffered(1))


def _rms(x, w):
    ms = jnp.mean(x * x, axis=-1, keepdims=True)
    return x * lax.rsqrt(ms + RMS_EPS) * w


def _silu(x):
    return x * jax.nn.sigmoid(x)


def _softplus(x):
    return jnp.maximum(x, 0.0) + jnp.log(1.0 + jnp.exp(-jnp.abs(x)))


def _dot(a, b):
    return jnp.dot(a, b, preferred_element_type=F32)


def _dot_nt(a, b):
    return lax.dot_general(a, b, (((1,), (1,)), ((), ())), preferred_element_type=F32)


def _ffn_body(x_ref, nw_ref, wg_ref, wu_ref, wd_ref, *rest, ff_chunk, final):
    if final:
        fw_ref, o_ref, a_ref = rest
    else:
        o_ref, a_ref = rest
    x = x_ref[...]
    h = _rms(x, nw_ref[...]).astype(BF16)
    for c in range(0, D_FF, ff_chunk):
        g = _dot(h, wg_ref[:, c:c + ff_chunk])
        u = _dot(h, wu_ref[:, c:c + ff_chunk])
        a_ref[:, c:c + ff_chunk] = (_silu(g) * u).astype(BF16)
    y = x + 0.5 * _dot(a_ref[...], wd_ref[...])
    if final:
        y = _rms(y, fw_ref[...])
    o_ref[...] = y


def _ffn(x, nw, wg, wu, wd, final_w=None, *, tm=512, ff_chunk=256):
    m, d = x.shape
    final = final_w is not None
    row = pl.BlockSpec((tm, d), lambda i: (i, 0))
    in_specs = [row, _resident((1, d)), _resident((d, D_FF)), _resident((d, D_FF)), _resident((D_FF, d))]
    args = [x, nw.reshape(1, d), wg.astype(BF16), wu.astype(BF16), wd.astype(BF16)]
    if final:
        in_specs.append(_resident((1, d)))
        args.append(final_w.reshape(1, d))
    return pl.pallas_call(
        functools.partial(_ffn_body, ff_chunk=ff_chunk, final=final),
        grid=(m // tm,),
        in_specs=in_specs,
        out_specs=row,
        out_shape=jax.ShapeDtypeStruct((m, d), F32),
        scratch_shapes=[pltpu.VMEM((tm, D_FF), BF16)],
        compiler_params=_cparams("parallel"),
        name="ffn",
    )(*args)


def _proj_out_body(*refs, n_lhs):
    x_ref = refs[0]
    o_ref = refs[-1]
    y = x_ref[...]
    for a_ref, w_ref in zip(refs[1:1 + n_lhs], refs[1 + n_lhs:1 + 2 * n_lhs]):
        y = y + _dot(a_ref[...], w_ref[...])
    o_ref[...] = y


def _proj_out(x, lhs, ws, *, tm=512):
    m, d = x.shape
    row = pl.BlockSpec((tm, d), lambda i: (i, 0))
    in_specs = [row]
    in_specs += [pl.BlockSpec((tm, a.shape[1]), lambda i: (i, 0)) for a in lhs]
    in_specs += [_resident(w.shape) for w in ws]
    return pl.pallas_call(
        functools.partial(_proj_out_body, n_lhs=len(lhs)),
        grid=(m // tm,),
        in_specs=in_specs,
        out_specs=row,
        out_shape=jax.ShapeDtypeStruct((m, d), F32),
        compiler_params=_cparams("parallel"),
        name="proj_out",
    )(x, *lhs, *[w.astype(BF16) for w in ws])


def _att_in_body(x_ref, nw_ref, w_ref, sc_ref, o_ref, *, n_chunk):
    h = _rms(x_ref[...], nw_ref[...]).astype(BF16)
    n = w_ref.shape[1]
    for c in range(0, n, n_chunk):
        p = _dot(h, w_ref[:, c:c + n_chunk])
        o_ref[:, c:c + n_chunk] = (p * sc_ref[:, c:c + n_chunk]).astype(BF16)


def _att_in(x, nw, w, col_scale, *, tm=512, n_chunk=512):
    m, d = x.shape
    n = w.shape[1]
    return pl.pallas_call(
        functools.partial(_att_in_body, n_chunk=n_chunk),
        grid=(m // tm,),
        in_specs=[pl.BlockSpec((tm, d), lambda i: (i, 0)), _resident((1, d)), _resident((d, n)),
                  _resident((1, n))],
        out_specs=pl.BlockSpec((tm, n), lambda i: (i, 0)),
        out_shape=jax.ShapeDtypeStruct((m, n), BF16),
        compiler_params=_cparams("parallel"),
        name="att_in",
    )(x, nw.reshape(1, d), w.astype(BF16), col_scale)


def _store_v_transposed(v_ref, vt_scr, tk):
    for j in range(vt_scr.shape[0]):
        vt_scr[j] = v_ref[j * tk:(j + 1) * tk, :].astype(F32).T.astype(BF16)


def _diff_body(slope_ref, lam_ref, q_ref, k_ref, v_ref, kpos_ref, qpos_ref, sw_ref, o_ref, vt_scr, *, tq, out_scale):
    h = pl.program_id(1)
    i = pl.program_id(2)
    slope2 = slope_ref[h]
    lam = lam_ref[0]

    @pl.when(i == 0)
    def _():
        _store_v_transposed(v_ref, vt_scr, tq)

    q = q_ref[...]
    lane = lax.broadcasted_iota(jnp.int32, q.shape, 1)
    zero = jnp.zeros_like(q)
    qpos = jnp.broadcast_to(qpos_ref[...], q.shape)
    q_cat = (jnp.concatenate([jnp.where(lane < DIFF_QK_DIM, q, zero), qpos], axis=1),
             jnp.concatenate([jnp.where(lane >= DIFF_QK_DIM, q, zero), qpos], axis=1))

    def block(j, carry, fixup):
        start = pl.multiple_of(j * tq, tq)
        k_cat = jnp.concatenate([k_ref[pl.ds(start, tq), :], kpos_ref[pl.ds(start, tq), :]], axis=1)
        vt = vt_scr[j]
        comps = range(2)
        m, l, acc = carry[0::3], carry[1::3], carry[2::3]
        s = [_dot_nt(k_cat, q_cat[c]) for c in comps]
        if fixup is not None:
            s = [s[c] + fixup for c in comps]
        m_new = [jnp.maximum(m[c], jnp.max(s[c], axis=0, keepdims=True)) for c in comps]
        alpha = [jnp.exp2(m[c] - m_new[c]) for c in comps]
        p = [jnp.exp2(s[c] - m_new[c]) for c in comps]
        l = [alpha[c] * l[c] + jnp.sum(p[c], axis=0, keepdims=True) for c in comps]
        pv = [_dot(vt, p[c].astype(BF16)) for c in comps]
        acc = [alpha[c] * acc[c] + pv[c] for c in comps]
        return (m_new[0], l[0], acc[0], m_new[1], l[1], acc[1])

    init = (jnp.full((1, tq), NEG_INF, F32), jnp.zeros((1, tq), F32), jnp.zeros((HEAD_COLS, tq), F32)) * 2
    carry = lax.fori_loop(0, i, lambda j, c: block(j, c, None), init)
    kj = lax.broadcasted_iota(jnp.int32, (tq, tq), 0)
    qi = lax.broadcasted_iota(jnp.int32, (tq, tq), 1)
    future = jnp.where(kj > qi, (2.0 * slope2) * (qi - kj).astype(F32), 0.0)
    fixup = jnp.where((kj >> MASK_CHUNK_LOG2) <= (qi >> MASK_CHUNK_LOG2), future, NEG_INF)
    m0, l0, a0, m1, l1, a1 = block(i, carry, fixup)
    o = (a0 * (1.0 / l0) - (lam / l1) * a1).T
    o_ref[...] = (_rms(o, sw_ref[...]) * out_scale).astype(BF16)


def _diff_attn(proj, slopes2, lam, subln, kpos, qpos, *, batch, seq, out_scale, tq=256):
    smem = pl.BlockSpec(memory_space=pltpu.SMEM)
    kv = lambda blk: pl.BlockSpec((None, seq, HEAD_COLS), lambda b, h, i: (b, 0, blk + h))
    return pl.pallas_call(
        functools.partial(_diff_body, tq=tq, out_scale=out_scale),
        grid=(batch, DIFF_HEADS, seq // tq),
        in_specs=[smem, smem,
                  pl.BlockSpec((None, tq, HEAD_COLS), lambda b, h, i: (b, i, h)),
                  kv(DIFF_HEADS), kv(2 * DIFF_HEADS),
                  pl.BlockSpec((seq, HEAD_COLS), lambda b, h, i: (0, 0)),
                  pl.BlockSpec((None, 1, HEAD_COLS), lambda b, h, i: (h, 0, 0)),
                  pl.BlockSpec((1, HEAD_COLS), lambda b, h, i: (0, 0))],
        out_specs=pl.BlockSpec((None, tq, HEAD_COLS), lambda b, h, i: (b, i, h)),
        out_shape=jax.ShapeDtypeStruct((batch, seq, DIFF_HEADS * HEAD_COLS), BF16),
        scratch_shapes=[pltpu.VMEM((seq // tq, HEAD_COLS, tq), BF16)],
        compiler_params=_cparams("parallel", "parallel", "arbitrary"),
        name="diff_attn",
    )(slopes2, lam, proj, proj, proj, kpos, qpos, subln.reshape(1, HEAD_COLS))


def _alibi_lanes(slopes2, seq):
    pos = jnp.arange(seq)
    a = (pos >> MASK_CHUNK_LOG2).astype(F32)
    b = (pos & ((1 << MASK_CHUNK_LOG2) - 1)).astype(F32)
    kpos = jnp.zeros((seq, HEAD_COLS), F32).at[:, 0:3].set(a[:, None]).at[:, 3:6].set(b[:, None]).astype(BF16)

    def split3(x):
        x1 = x.astype(BF16).astype(F32)
        x2 = (x - x1).astype(BF16).astype(F32)
        return jnp.stack([x1, x2, (x - x1 - x2).astype(BF16).astype(F32)], axis=-1)

    mult = jnp.concatenate([split3(slopes2 * (1 << MASK_CHUNK_LOG2)), split3(slopes2)], axis=-1)
    qpos = jnp.zeros((slopes2.shape[0], 1, HEAD_COLS), F32).at[:, 0, 0:6].set(mult).astype(BF16)
    return kpos, qpos


SB_GROUP = 128


def _sb_body(q_ref, k_ref, v_ref, o_ref, vt_scr, *, tq):
    i = pl.program_id(2)

    @pl.when(i == 0)
    def _():
        _store_v_transposed(v_ref, vt_scr, tq)

    q = q_ref[...]
    gs = lax.broadcasted_iota(jnp.int32, (SB_GROUP, SB_GROUP), 0)
    gj = lax.broadcasted_iota(jnp.int32, (SB_GROUP, SB_GROUP), 1)
    later = (gj > gs).astype(BF16)
    later2 = jnp.concatenate([later, later], axis=1)
    n_groups = tq // SB_GROUP

    def block(j, carry, earlier):
        tail, acc = carry
        start = pl.multiple_of(j * tq, tq)
        z = _dot_nt(k_ref[pl.ds(start, tq), :], q)
        sp = jnp.log(1.0 + jnp.exp(-jnp.abs(z)))
        log_beta = jnp.minimum(z, 0.0) - sp
        neg_keep = jnp.maximum(z, 0.0) + sp
        if earlier is not None:
            neg_keep = jnp.where(earlier, neg_keep, 0.0)
        after = []
        for g in range(n_groups - 1, -1, -1):
            x = neg_keep[g * SB_GROUP:(g + 1) * SB_GROUP, :]
            hi, lo = _hi_lo(x)
            within = _dot(later2, jnp.concatenate([hi, lo], axis=0))
            after.append(within + tail)
            tail = tail + jnp.sum(x, axis=0, keepdims=True)
        a = jnp.exp(log_beta - jnp.concatenate(after[::-1], axis=0))
        if earlier is not None:
            a = jnp.where(earlier, a, 0.0)
        acc = acc + _dot(vt_scr[j], a.astype(BF16))
        return tail, acc

    kj = lax.broadcasted_iota(jnp.int32, (tq, tq), 0)
    qi = lax.broadcasted_iota(jnp.int32, (tq, tq), 1)
    init = (jnp.zeros((1, tq), F32), jnp.zeros((HEAD_COLS, tq), F32))
    carry = block(i, init, kj < qi)
    _, acc = lax.fori_loop(0, i, lambda t, c: block(i - 1 - t, c, None), carry)
    o_ref[...] = acc.T.astype(BF16)


def _sb_attn(proj, *, batch, seq, tq=256):
    base = 3 * DIFF_HEADS
    kv = lambda blk: pl.BlockSpec((None, seq, HEAD_COLS), lambda b, h, i: (b, 0, blk + h))
    return pl.pallas_call(
        functools.partial(_sb_body, tq=tq),
        grid=(batch, SB_HEADS, seq // tq),
        in_specs=[pl.BlockSpec((None, tq, HEAD_COLS), lambda b, h, i: (b, i, base + h)),
                  kv(base + SB_HEADS), kv(base + 2 * SB_HEADS)],
        out_specs=pl.BlockSpec((None, tq, HEAD_COLS), lambda b, h, i: (b, i, h)),
        out_shape=jax.ShapeDtypeStruct((batch, seq, SB_HEADS * HEAD_COLS), BF16),
        scratch_shapes=[pltpu.VMEM((seq // tq, HEAD_COLS, tq), BF16)],
        compiler_params=_cparams("parallel", "parallel", "arbitrary"),
        name="sb_attn",
    )(proj, proj, proj)


def _gdn_in_body(x_ref, xp_ref, nw_ref, wqkv_ref, wg_ref, wba_ref, cw_ref, alog_ref, dt_ref,
                 q_ref, k_ref, v_ref, gate_ref, bg_ref, p_scr, *, tm, seq, n_chunk):
    i = pl.program_id(0)
    nw = nw_ref[...]
    h = _rms(x_ref[...], nw).astype(BF16)
    seq_start = (i * tm) % seq == 0
    hp = jnp.where(seq_start, 0.0, _rms(xp_ref[...], nw)).astype(BF16)
    mix = GDN_HEADS * HEAD_COLS
    outs = (q_ref, k_ref, v_ref)
    q_scale = HEAD_COLS ** -0.5
    for c in range(0, 3 * mix, n_chunk):
        p_scr[0:CONV_HALO, :] = _dot(hp, wqkv_ref[:, c:c + n_chunk])
        p_scr[CONV_HALO:CONV_HALO + tm, :] = _dot(h, wqkv_ref[:, c:c + n_chunk])
        for s in range(0, n_chunk, HEAD_COLS):
            col = c + s
            y = None
            for tap in range(GDN_CONV):
                shift = CONV_HALO - (GDN_CONV - 1) + tap
                term = p_scr[shift:shift + tm, s:s + HEAD_COLS] * cw_ref[tap:tap + 1, col:col + HEAD_COLS]
                y = term if y is None else y + term
            y = _silu(y)
            which, local = divmod(col, mix)
            if which < 2:
                y = y * lax.rsqrt(jnp.sum(y * y, axis=-1, keepdims=True) + L2_EPS)
            if which == 0:
                y = y * q_scale
            outs[which][:, local:local + HEAD_COLS] = y.astype(BF16)
    for c in range(0, mix, n_chunk):
        gate_ref[:, c:c + n_chunk] = _dot(h, wg_ref[:, c:c + n_chunk]).astype(BF16)
    ba = _dot(h, wba_ref[...])
    lane = lax.broadcasted_iota(jnp.int32, ba.shape, 1)
    beta = jax.nn.sigmoid(ba)
    g = -jnp.exp(alog_ref[...]) * _softplus(ba + dt_ref[...])
    bg_ref[...] = jnp.where(lane < GDN_HEADS, beta, jnp.where(lane < 2 * GDN_HEADS, g, 0.0))


def _gdn_in(x, nw, w_in, conv_w, a_log, dt_bias, *, seq, tm=512, n_chunk=512):
    m, d = x.shape
    mix = GDN_HEADS * HEAD_COLS
    w_qkv = w_in[:, :3 * mix].astype(BF16)
    w_gate = w_in[:, 3 * mix:4 * mix].astype(BF16)
    pad = HEAD_COLS - 2 * GDN_HEADS
    w_ba = jnp.pad(w_in[:, 4 * mix:], ((0, 0), (0, pad))).astype(BF16)
    lanes = lambda v: jnp.pad(v.astype(F32), (GDN_HEADS, pad)).reshape(1, HEAD_COLS)
    row = lambda n: pl.BlockSpec((tm, n), lambda i: (i, 0))
    halo_blocks = tm // CONV_HALO
    act = jax.ShapeDtypeStruct((m, mix), BF16)
    return pl.pallas_call(
        functools.partial(_gdn_in_body, tm=tm, seq=seq, n_chunk=n_chunk),
        grid=(m // tm,),
        in_specs=[row(d),
                  pl.BlockSpec((CONV_HALO, d), lambda i: (jnp.maximum(i * halo_blocks - 1, 0), 0)),
                  _resident((1, d)), _resident((d, 3 * mix)), _resident((d, mix)), _resident((d, HEAD_COLS)),
                  _resident((GDN_CONV, 3 * mix)), _resident((1, HEAD_COLS)), _resident((1, HEAD_COLS))],
        out_specs=[row(mix), row(mix), row(mix), row(mix), row(HEAD_COLS)],
        out_shape=[act, act, act, act, jax.ShapeDtypeStruct((m, HEAD_COLS), F32)],
        scratch_shapes=[pltpu.VMEM((CONV_HALO + tm, n_chunk), F32)],
        compiler_params=_cparams("parallel"),
        name="gdn_in",
    )(x, x, nw.reshape(1, d), w_qkv, w_gate, w_ba, conv_w.astype(F32), lanes(a_log), lanes(dt_bias))


def _split3(x):
    b1 = x.astype(BF16)
    r1 = x - b1.astype(F32)
    b2 = r1.astype(BF16)
    b3 = (r1 - b2.astype(F32)).astype(BF16)
    return b1, b2, b3


def _hi_lo(x):
    hi = x.astype(BF16)
    return hi, (x - hi.astype(F32)).astype(BF16)


def _dot_split(a_parts, b_parts):
    ah, al = a_parts
    bh, bl = b_parts
    c = bh.shape[0]
    rhs = jnp.concatenate([jnp.concatenate([bh, bl], axis=1),
                           jnp.concatenate([bh, jnp.zeros_like(bh)], axis=1)], axis=0)
    r = _dot(jnp.concatenate([ah, al], axis=1), rhs)
    return r[:, :c] + r[:, c:]


def _unit_lower_inverse(lows, eye):
    c = lows[0].shape[0]
    n = range(len(lows))
    low_parts = [_hi_lo(low) for low in lows]
    pw = [_dot_split(low_parts[i], low_parts[i]) for i in n]
    t_inv = [eye - low for low in lows]
    for _ in range(int(math.log2(c)) - 2):
        pw_parts = [_hi_lo(p) for p in pw]
        t_parts = [_hi_lo(t) for t in t_inv]
        stacked = [tuple(jnp.concatenate([t, p], axis=0) for t, p in zip(t_parts[i], pw_parts[i])) for i in n]
        both = [_dot_split(stacked[i], pw_parts[i]) for i in n]
        t_inv = [t_inv[i] + both[i][:c] for i in n]
        pw = [both[i][c:] for i in n]
    return [t_inv[i] + _dot_split(_hi_lo(t_inv[i]), _hi_lo(pw[i])) for i in n]


def _gdn_body(q_ref, k_ref, v_ref, gate_ref, bg_ref, nw_ref, o_ref, s_scr):
    c = GDN_CHUNK

    @pl.when(pl.program_id(1) == 0)
    def _():
        s_scr[...] = jnp.zeros_like(s_scr)

    row = lax.broadcasted_iota(jnp.int32, (c, c), 0)
    col = lax.broadcasted_iota(jnp.int32, (c, c), 1)
    incl = col <= row
    strict = col < row
    eye = (col == row).astype(F32)
    tri = incl.astype(BF16)
    bg = bg_ref[...]
    gc_all = sum(_dot(tri, part) for part in _split3(bg))
    gc_rows = gc_all.T
    nw = nw_ref[...]
    heads = range(GDN_HEADS)
    sl = [slice(h * HEAD_COLS, (h + 1) * HEAD_COLS) for h in heads]
    state = [s_scr[h] for h in heads]
    q = [q_ref[:, sl[h]].astype(F32) for h in heads]
    k = [k_ref[:, sl[h]].astype(F32) for h in heads]
    v = [v_ref[:, sl[h]].astype(F32) for h in heads]
    beta = [bg[:, h:h + 1] for h in heads]
    gcb = [jnp.broadcast_to(gc_all[:, GDN_HEADS + h:GDN_HEADS + h + 1], (c, c)) for h in heads]
    gct = [jnp.broadcast_to(gc_rows[GDN_HEADS + h:GDN_HEADS + h + 1, :], (c, c)) for h in heads]
    decay = [jnp.exp(jnp.where(incl, gcb[h] - gct[h], NEG_INF)) for h in heads]
    eg = [jnp.exp(gcb[h]) for h in heads]
    g_last = [gcb[h][c - 1:c, :] for h in heads]
    kb = [k[h] * beta[h] for h in heads]
    kbf = [k[h].astype(BF16) for h in heads]
    low = [jnp.where(strict, _dot_nt(kb[h].astype(BF16), kbf[h]) * decay[h], 0.0) for h in heads]
    aqk = [jnp.where(incl, _dot_nt(q[h].astype(BF16), kbf[h]) * decay[h], 0.0).astype(BF16) for h in heads]
    tb = [t.astype(BF16) for t in _unit_lower_inverse(low, eye)]
    u = [_dot(tb[h], (v[h] * beta[h]).astype(BF16)) for h in heads]
    w = [_dot(tb[h], (kb[h] * eg[h]).astype(BF16)).astype(BF16) for h in heads]
    sb = [state[h].astype(BF16) for h in heads]
    vnb = [(u[h] - _dot(w[h], sb[h])).astype(BF16) for h in heads]
    o = [_dot((q[h] * eg[h]).astype(BF16), sb[h]) + _dot(aqk[h], vnb[h]) for h in heads]
    k_dec = [(k[h] * jnp.exp(g_last[h] - gcb[h])).T.astype(BF16) for h in heads]
    new_state = [state[h] * jnp.exp(g_last[h]) + _dot(k_dec[h], vnb[h]) for h in heads]
    for h in heads:
        s_scr[h] = new_state[h]
        o_ref[:, sl[h]] = (_rms(o[h], nw) * _silu(gate_ref[:, sl[h]].astype(F32))).astype(BF16)


def _gdn(q, k, v, gate, bg, norm_w, *, batch, seq):
    mix = GDN_HEADS * HEAD_COLS
    c = GDN_CHUNK
    per_seq = seq // c
    row = lambda n: pl.BlockSpec((c, n), lambda b, t: (b * per_seq + t, 0))
    return pl.pallas_call(
        _gdn_body,
        grid=(batch, per_seq),
        in_specs=[row(mix), row(mix), row(mix), row(mix), row(HEAD_COLS),
                  pl.BlockSpec((1, HEAD_COLS), lambda b, t: (0, 0))],
        out_specs=row(mix),
        out_shape=jax.ShapeDtypeStruct((batch * seq, mix), BF16),
        scratch_shapes=[pltpu.VMEM((GDN_HEADS, HEAD_COLS, HEAD_COLS), F32)],
        compiler_params=_cparams("parallel", "arbitrary"),
        name="gdn",
    )(q, k, v, gate, bg, norm_w.reshape(1, HEAD_COLS))


def _even_mixer(x, norm_w, w_in, diff_lambda, diff_subln, w_out, layer_idx, *, batch, seq):
    n_in = w_in.shape[1]
    q_cols = DIFF_HEADS * HEAD_COLS
    sb_q0 = 3 * q_cols
    col = jnp.arange(n_in)
    log2e = math.log2(math.e)
    col_scale = jnp.where(col < q_cols, DIFF_QK_DIM ** -0.5 * log2e,
                          jnp.where((col >= sb_q0) & (col < sb_q0 + SB_HEADS * HEAD_COLS), HEAD_COLS ** -0.5, 1.0))
    proj = _att_in(x, norm_w, w_in, col_scale.astype(F32).reshape(1, n_in)).reshape(batch, seq, n_in)
    lambda_init = 0.8 - 0.6 * math.exp(-0.3 * layer_idx)
    lp = diff_lambda.astype(F32)
    lam = jnp.exp(jnp.sum(lp[0] * lp[1])) - jnp.exp(jnp.sum(lp[2] * lp[3])) + lambda_init
    slopes2 = 2.0 ** (-8.0 * jnp.arange(1, DIFF_HEADS + 1, dtype=F32) / DIFF_HEADS) * log2e
    kpos, qpos = _alibi_lanes(slopes2, seq)
    oa = _diff_attn(proj, slopes2, lam.reshape(1), diff_subln.astype(F32), kpos, qpos, batch=batch, seq=seq,
                    out_scale=1.0 - lambda_init)
    osb = _sb_attn(proj, batch=batch, seq=seq)
    m = batch * seq
    return _proj_out(x, [oa.reshape(m, -1), osb.reshape(m, -1)], [w_out[:q_cols], w_out[q_cols:]])


def _odd_mixer(x, norm_w, w_in, conv_w, a_log, dt_bias, gdn_norm, w_out, *, batch, seq):
    q, k, v, gate, bg = _gdn_in(x, norm_w, w_in, conv_w, a_log, dt_bias, seq=seq)
    o = _gdn(q, k, v, gate, bg, gdn_norm.astype(F32), batch=batch, seq=seq)
    return _proj_out(x, [o], [w_out])


def kernel(x, ffn1_norm, ffn1_w_gate, ffn1_w_up, ffn1_w_down, mix_norm, att_w_in, diff_lambda, diff_subln, att_w_out, gdn_w_in, gdn_conv_w, gdn_a_log, gdn_dt_bias, gdn_norm, gdn_w_out, ffn2_norm, ffn2_w_gate, ffn2_w_up, ffn2_w_down, final_norm):
    batch, seq, d = x.shape
    depth = ffn1_norm.shape[0]
    xf = x.reshape(batch * seq, d)
    for layer in range(depth):
        xf = _ffn(xf, ffn1_norm[layer], ffn1_w_gate[layer], ffn1_w_up[layer], ffn1_w_down[layer])
        if layer % 2 == 0:
            e = layer // 2
            xf = _even_mixer(xf, mix_norm[layer], att_w_in[e], diff_lambda[e], diff_subln[e], att_w_out[e], layer,
                             batch=batch, seq=seq)
        else:
            o = layer // 2
            xf = _odd_mixer(xf, mix_norm[layer], gdn_w_in[o], gdn_conv_w[o], gdn_a_log[o], gdn_dt_bias[o],
                            gdn_norm[o], gdn_w_out[o], batch=batch, seq=seq)
        last = layer == depth - 1
        xf = _ffn(xf, ffn2_norm[layer], ffn2_w_gate[layer], ffn2_w_up[layer], ffn2_w_down[layer],
                  final_w=final_norm if last else None)
    return xf.reshape(batch, seq, d)
```

```python
import functools
import math

import jax
import jax.numpy as jnp
from jax import lax
from jax.experimental import pallas as pl
from jax.experimental.pallas import tpu as pltpu

F32 = jnp.float32
BF16 = jnp.bfloat16

D_FF = 2816
RMS_EPS = 1e-6
L2_EPS = 1e-6
NEG_INF = -1e30

MASK_CHUNK_LOG2 = 6
DIFF_HEADS = 4
DIFF_QK_DIM = 64
SB_HEADS = 4
HEAD_COLS = 128
GDN_HEADS = 8
GDN_CONV = 4
GDN_CHUNK = 128

V7X_VMEM_LIMIT_BYTES = 56 * 1024 * 1024
CONV_HALO = 16


def _cparams(*semantics):
    return pltpu.CompilerParams(dimension_semantics=semantics, vmem_limit_bytes=V7X_VMEM_LIMIT_BYTES)


def _resident(shape):
    zeros = (0,) * len(shape)
    return pl.BlockSpec(shape, lambda *_: zeros, pipeline_mode=pl.Buffered(1))


def _rms(x, w):
    ms = jnp.mean(x * x, axis=-1, keepdims=True)
    return x * lax.rsqrt(ms + RMS_EPS) * w


def _silu(x):
    return x * jax.nn.sigmoid(x)


def _softplus(x):
    return jnp.maximum(x, 0.0) + jnp.log(1.0 + jnp.exp(-jnp.abs(x)))


def _dot(a, b):
    return jnp.dot(a, b, preferred_element_type=F32)


def _dot_nt(a, b):
    return lax.dot_general(a, b, (((1,), (1,)), ((), ())), preferred_element_type=F32)


def _ffn_body(x_ref, nw_ref, wg_ref, wu_ref, wd_ref, *rest, ff_chunk, final):
    if final:
        fw_ref, o_ref, a_ref = rest
    else:
        o_ref, a_ref = rest
    x = x_ref[...]
    h = _rms(x, nw_ref[...]).astype(BF16)
    for c in range(0, D_FF, ff_chunk):
        g = _dot(h, wg_ref[:, c:c + ff_chunk])
        u = _dot(h, wu_ref[:, c:c + ff_chunk])
        a_ref[:, c:c + ff_chunk] = (_silu(g) * u).astype(BF16)
    y = x + 0.5 * _dot(a_ref[...], wd_ref[...])
    if final:
        y = _rms(y, fw_ref[...])
    o_ref[...] = y


def _ffn(x, nw, wg, wu, wd, final_w=None, *, tm=512, ff_chunk=256):
    m, d = x.shape
    final = final_w is not None
    row = pl.BlockSpec((tm, d), lambda i: (i, 0))
    in_specs = [row, _resident((1, d)), _resident((d, D_FF)), _resident((d, D_FF)), _resident((D_FF, d))]
    args = [x, nw.reshape(1, d), wg.astype(BF16), wu.astype(BF16), wd.astype(BF16)]
    if final:
        in_specs.append(_resident((1, d)))
        args.append(final_w.reshape(1, d))
    return pl.pallas_call(
        functools.partial(_ffn_body, ff_chunk=ff_chunk, final=final),
        grid=(m // tm,),
        in_specs=in_specs,
        out_specs=row,
        out_shape=jax.ShapeDtypeStruct((m, d), F32),
        scratch_shapes=[pltpu.VMEM((tm, D_FF), BF16)],
        compiler_params=_cparams("parallel"),
        name="ffn",
    )(*args)


def _proj_out_body(*refs, n_lhs):
    x_ref = refs[0]
    o_ref = refs[-1]
    y = x_ref[...]
    for a_ref, w_ref in zip(refs[1:1 + n_lhs], refs[1 + n_lhs:1 + 2 * n_lhs]):
        y = y + _dot(a_ref[...], w_ref[...])
    o_ref[...] = y


def _proj_out(x, lhs, ws, *, tm=512):
    m, d = x.shape
    row = pl.BlockSpec((tm, d), lambda i: (i, 0))
    in_specs = [row]
    in_specs += [pl.BlockSpec((tm, a.shape[1]), lambda i: (i, 0)) for a in lhs]
    in_specs += [_resident(w.shape) for w in ws]
    return pl.pallas_call(
        functools.partial(_proj_out_body, n_lhs=len(lhs)),
        grid=(m // tm,),
        in_specs=in_specs,
        out_specs=row,
        out_shape=jax.ShapeDtypeStruct((m, d), F32),
        compiler_params=_cparams("parallel"),
        name="proj_out",
    )(x, *lhs, *[w.astype(BF16) for w in ws])


def _att_in_body(x_ref, nw_ref, w_ref, sc_ref, o_ref, *, n_chunk):
    h = _rms(x_ref[...], nw_ref[...]).astype(BF16)
    n = w_ref.shape[1]
    for c in range(0, n, n_chunk):
        p = _dot(h, w_ref[:, c:c + n_chunk])
        o_ref[:, c:c + n_chunk] = (p * sc_ref[:, c:c + n_chunk]).astype(BF16)


def _att_in(x, nw, w, col_scale, *, tm=512, n_chunk=512):
    m, d = x.shape
    n = w.shape[1]
    return pl.pallas_call(
        functools.partial(_att_in_body, n_chunk=n_chunk),
        grid=(m // tm,),
        in_specs=[pl.BlockSpec((tm, d), lambda i: (i, 0)), _resident((1, d)), _resident((d, n)),
                  _resident((1, n))],
        out_specs=pl.BlockSpec((tm, n), lambda i: (i, 0)),
        out_shape=jax.ShapeDtypeStruct((m, n), BF16),
        compiler_params=_cparams("parallel"),
        name="att_in",
    )(x, nw.reshape(1, d), w.astype(BF16), col_scale)


def _head_cols(h):
    return slice(h * HEAD_COLS, (h + 1) * HEAD_COLS)


def _store_v_transposed(v_ref, vt_scr, tk):
    for h in range(vt_scr.shape[0]):
        for j in range(vt_scr.shape[1]):
            vt_scr[h, j] = v_ref[j * tk:(j + 1) * tk, _head_cols(h)].astype(F32).T.astype(BF16)


def _diff_body(slope_ref, lam_ref, q_ref, k_ref, v_ref, kpos_ref, qpos_ref, sw_ref, o_ref, vt_scr, *, tq, out_scale):
    i = pl.program_id(1)
    lam = lam_ref[0]
    heads = range(DIFF_HEADS)
    chains = [(h, c) for h in heads for c in range(2)]

    @pl.when(i == 0)
    def _():
        _store_v_transposed(v_ref, vt_scr, tq)

    lane = lax.broadcasted_iota(jnp.int32, (tq, HEAD_COLS), 1)
    first = lane < DIFF_QK_DIM
    q_cat = []
    for h in heads:
        q = q_ref[:, _head_cols(h)]
        zero = jnp.zeros_like(q)
        qpos = jnp.broadcast_to(qpos_ref[h], q.shape)
        q_cat += [jnp.concatenate([jnp.where(first, q, zero), qpos], axis=1),
                  jnp.concatenate([jnp.where(first, zero, q), qpos], axis=1)]

    def block(j, carry, fixups):
        n = range(len(chains))
        start = pl.multiple_of(j * tq, tq)
        kpos = kpos_ref[pl.ds(start, tq), :]
        k_cat = [jnp.concatenate([k_ref[pl.ds(start, tq), _head_cols(h)], kpos], axis=1) for h in heads]
        vt = [vt_scr[h, j] for h in heads]
        m, l, acc = carry[0::3], carry[1::3], carry[2::3]
        s = [_dot_nt(k_cat[h], q_cat[x]) for x, (h, _) in enumerate(chains)]
        if fixups is not None:
            s = [s[x] + fixups[h] for x, (h, _) in enumerate(chains)]
        m_new = [jnp.maximum(m[x], jnp.max(s[x], axis=0, keepdims=True)) for x in n]
        alpha = [jnp.exp2(m[x] - m_new[x]) for x in n]
        p = [jnp.exp2(s[x] - m_new[x]) for x in n]
        l = [alpha[x] * l[x] + jnp.sum(p[x], axis=0, keepdims=True) for x in n]
        pv = [_dot(vt[h], p[x].astype(BF16)) for x, (h, _) in enumerate(chains)]
        acc = [alpha[x] * acc[x] + pv[x] for x in n]
        return tuple(v for x in n for v in (m_new[x], l[x], acc[x]))

    init = (jnp.full((1, tq), NEG_INF, F32), jnp.zeros((1, tq), F32),
            jnp.zeros((HEAD_COLS, tq), F32)) * len(chains)
    carry = lax.fori_loop(0, i, lambda j, c: block(j, c, None), init)
    kj = lax.broadcasted_iota(jnp.int32, (tq, tq), 0)
    qi = lax.broadcasted_iota(jnp.int32, (tq, tq), 1)
    ahead = jnp.where(kj > qi, (qi - kj).astype(F32), 0.0)
    allowed = (kj >> MASK_CHUNK_LOG2) <= (qi >> MASK_CHUNK_LOG2)
    fixups = [jnp.where(allowed, (2.0 * slope_ref[h]) * ahead, NEG_INF) for h in heads]
    out = block(i, carry, fixups)
    sw = sw_ref[...]
    for h in heads:
        _, l0, a0, _, l1, a1 = out[6 * h:6 * h + 6]
        o = (a0 * (1.0 / l0) - (lam / l1) * a1).T
        o_ref[:, _head_cols(h)] = (_rms(o, sw) * out_scale).astype(BF16)


def _diff_attn(proj, slopes2, lam, subln, kpos, qpos, *, batch, seq, out_scale, tq=256):
    smem = pl.BlockSpec(memory_space=pltpu.SMEM)
    width = DIFF_HEADS * HEAD_COLS
    kv = lambda blk: pl.BlockSpec((None, seq, width), lambda b, i: (b, 0, blk))
    return pl.pallas_call(
        functools.partial(_diff_body, tq=tq, out_scale=out_scale),
        grid=(batch, seq // tq),
        in_specs=[smem, smem,
                  pl.BlockSpec((None, tq, width), lambda b, i: (b, i, 0)),
                  kv(1), kv(2),
                  pl.BlockSpec((seq, HEAD_COLS), lambda b, i: (0, 0)),
                  pl.BlockSpec((DIFF_HEADS, 1, HEAD_COLS), lambda b, i: (0, 0, 0)),
                  pl.BlockSpec((1, HEAD_COLS), lambda b, i: (0, 0))],
        out_specs=pl.BlockSpec((None, tq, width), lambda b, i: (b, i, 0)),
        out_shape=jax.ShapeDtypeStruct((batch, seq, width), BF16),
        scratch_shapes=[pltpu.VMEM((DIFF_HEADS, seq // tq, HEAD_COLS, tq), BF16)],
        compiler_params=_cparams("parallel", "arbitrary"),
        name="diff_attn",
    )(slopes2, lam, proj, proj, proj, kpos, qpos, subln.reshape(1, HEAD_COLS))


def _alibi_lanes(slopes2, seq):
    pos = jnp.arange(seq)
    a = (pos >> MASK_CHUNK_LOG2).astype(F32)
    b = (pos & ((1 << MASK_CHUNK_LOG2) - 1)).astype(F32)
    kpos = jnp.zeros((seq, HEAD_COLS), F32).at[:, 0:3].set(a[:, None]).at[:, 3:6].set(b[:, None]).astype(BF16)

    def split3(x):
        x1 = x.astype(BF16).astype(F32)
        x2 = (x - x1).astype(BF16).astype(F32)
        return jnp.stack([x1, x2, (x - x1 - x2).astype(BF16).astype(F32)], axis=-1)

    mult = jnp.concatenate([split3(slopes2 * (1 << MASK_CHUNK_LOG2)), split3(slopes2)], axis=-1)
    qpos = jnp.zeros((slopes2.shape[0], 1, HEAD_COLS), F32).at[:, 0, 0:6].set(mult).astype(BF16)
    return kpos, qpos


SB_GROUP = 128


def _sb_body(q_ref, k_ref, v_ref, o_ref, vt_scr, *, tq):
    i = pl.program_id(1)
    heads = range(SB_HEADS)

    @pl.when(i == 0)
    def _():
        _store_v_transposed(v_ref, vt_scr, tq)

    q = [q_ref[:, _head_cols(h)] for h in heads]
    gs = lax.broadcasted_iota(jnp.int32, (SB_GROUP, SB_GROUP), 0)
    gj = lax.broadcasted_iota(jnp.int32, (SB_GROUP, SB_GROUP), 1)
    later = (gj > gs).astype(BF16)
    later2 = jnp.concatenate([later, later], axis=1)
    groups = [slice(g * SB_GROUP, (g + 1) * SB_GROUP) for g in range(tq // SB_GROUP)]

    def block(j, carry, earlier):
        tail, acc = list(carry[0::2]), carry[1::2]
        start = pl.multiple_of(j * tq, tq)
        z = [_dot_nt(k_ref[pl.ds(start, tq), _head_cols(h)], q[h]) for h in heads]
        sp = [jnp.log(1.0 + jnp.exp(-jnp.abs(z[h]))) for h in heads]
        log_beta = [jnp.minimum(z[h], 0.0) - sp[h] for h in heads]
        neg_keep = [jnp.maximum(z[h], 0.0) + sp[h] for h in heads]
        if earlier is not None:
            neg_keep = [jnp.where(earlier, neg_keep[h], 0.0) for h in heads]
        after = [[None] * len(groups) for _ in heads]
        for g in reversed(range(len(groups))):
            x = [neg_keep[h][groups[g], :] for h in heads]
            parts = [_hi_lo(x[h]) for h in heads]
            within = [_dot(later2, jnp.concatenate(parts[h], axis=0)) for h in heads]
            for h in heads:
                after[h][g] = within[h] + tail[h]
                tail[h] = tail[h] + jnp.sum(x[h], axis=0, keepdims=True)
        a = [jnp.exp(log_beta[h] - jnp.concatenate(after[h], axis=0)) for h in heads]
        if earlier is not None:
            a = [jnp.where(earlier, a[h], 0.0) for h in heads]
        acc = [acc[h] + _dot(vt_scr[h, j], a[h].astype(BF16)) for h in heads]
        return tuple(v for h in heads for v in (tail[h], acc[h]))

    kj = lax.broadcasted_iota(jnp.int32, (tq, tq), 0)
    qi = lax.broadcasted_iota(jnp.int32, (tq, tq), 1)
    init = (jnp.zeros((1, tq), F32), jnp.zeros((HEAD_COLS, tq), F32)) * SB_HEADS
    carry = block(i, init, kj < qi)
    out = lax.fori_loop(0, i, lambda t, c: block(i - 1 - t, c, None), carry)
    for h in heads:
        o_ref[:, _head_cols(h)] = out[2 * h + 1].T.astype(BF16)


def _sb_attn(proj, *, batch, seq, tq=256):
    width = SB_HEADS * HEAD_COLS
    kv = lambda blk: pl.BlockSpec((None, seq, width), lambda b, i: (b, 0, blk))
    return pl.pallas_call(
        functools.partial(_sb_body, tq=tq),
        grid=(batch, seq // tq),
        in_specs=[pl.BlockSpec((None, tq, width), lambda b, i: (b, i, 3)), kv(4), kv(5)],
        out_specs=pl.BlockSpec((None, tq, width), lambda b, i: (b, i, 0)),
        out_shape=jax.ShapeDtypeStruct((batch, seq, width), BF16),
        scratch_shapes=[pltpu.VMEM((SB_HEADS, seq // tq, HEAD_COLS, tq), BF16)],
        compiler_params=_cparams("parallel", "arbitrary"),
        name="sb_attn",
    )(proj, proj, proj)


def _gdn_in_body(x_ref, xp_ref, nw_ref, wqkv_ref, wg_ref, wba_ref, cw_ref, alog_ref, dt_ref,
                 q_ref, k_ref, v_ref, gate_ref, bg_ref, p_scr, *, tm, seq, n_chunk):
    i = pl.program_id(0)
    nw = nw_ref[...]
    h = _rms(x_ref[...], nw).astype(BF16)
    seq_start = (i * tm) % seq == 0
    hp = jnp.where(seq_start, 0.0, _rms(xp_ref[...], nw)).astype(BF16)
    mix = GDN_HEADS * HEAD_COLS
    outs = (q_ref, k_ref, v_ref)
    q_scale = HEAD_COLS ** -0.5
    for c in range(0, 3 * mix, n_chunk):
        p_scr[0:CONV_HALO, :] = _dot(hp, wqkv_ref[:, c:c + n_chunk])
        p_scr[CONV_HALO:CONV_HALO + tm, :] = _dot(h, wqkv_ref[:, c:c + n_chunk])
        for s in range(0, n_chunk, HEAD_COLS):
            col = c + s
            y = None
            for tap in range(GDN_CONV):
                shift = CONV_HALO - (GDN_CONV - 1) + tap
                term = p_scr[shift:shift + tm, s:s + HEAD_COLS] * cw_ref[tap:tap + 1, col:col + HEAD_COLS]
                y = term if y is None else y + term
            y = _silu(y)
            which, local = divmod(col, mix)
            if which < 2:
                y = y * lax.rsqrt(jnp.sum(y * y, axis=-1, keepdims=True) + L2_EPS)
            if which == 0:
                y = y * q_scale
            outs[which][:, local:local + HEAD_COLS] = y.astype(BF16)
    for c in range(0, mix, n_chunk):
        gate_ref[:, c:c + n_chunk] = _dot(h, wg_ref[:, c:c + n_chunk]).astype(BF16)
    ba = _dot(h, wba_ref[...])
    lane = lax.broadcasted_iota(jnp.int32, ba.shape, 1)
    beta = jax.nn.sigmoid(ba)
    g = -jnp.exp(alog_ref[...]) * _softplus(ba + dt_ref[...])
    bg_ref[...] = jnp.where(lane < GDN_HEADS, beta, jnp.where(lane < 2 * GDN_HEADS, g, 0.0))


def _gdn_in(x, nw, w_in, conv_w, a_log, dt_bias, *, seq, tm=512, n_chunk=512):
    m, d = x.shape
    mix = GDN_HEADS * HEAD_COLS
    w_qkv = w_in[:, :3 * mix].astype(BF16)
    w_gate = w_in[:, 3 * mix:4 * mix].astype(BF16)
    pad = HEAD_COLS - 2 * GDN_HEADS
    w_ba = jnp.pad(w_in[:, 4 * mix:], ((0, 0), (0, pad))).astype(BF16)
    lanes = lambda v: jnp.pad(v.astype(F32), (GDN_HEADS, pad)).reshape(1, HEAD_COLS)
    row = lambda n: pl.BlockSpec((tm, n), lambda i: (i, 0))
    halo_blocks = tm // CONV_HALO
    act = jax.ShapeDtypeStruct((m, mix), BF16)
    return pl.pallas_call(
        functools.partial(_gdn_in_body, tm=tm, seq=seq, n_chunk=n_chunk),
        grid=(m // tm,),
        in_specs=[row(d),
                  pl.BlockSpec((CONV_HALO, d), lambda i: (jnp.maximum(i * halo_blocks - 1, 0), 0)),
                  _resident((1, d)), _resident((d, 3 * mix)), _resident((d, mix)), _resident((d, HEAD_COLS)),
                  _resident((GDN_CONV, 3 * mix)), _resident((1, HEAD_COLS)), _resident((1, HEAD_COLS))],
        out_specs=[row(mix), row(mix), row(mix), row(mix), row(HEAD_COLS)],
        out_shape=[act, act, act, act, jax.ShapeDtypeStruct((m, HEAD_COLS), F32)],
        scratch_shapes=[pltpu.VMEM((CONV_HALO + tm, n_chunk), F32)],
        compiler_params=_cparams("parallel"),
        name="gdn_in",
    )(x, x, nw.reshape(1, d), w_qkv, w_gate, w_ba, conv_w.astype(F32), lanes(a_log), lanes(dt_bias))


def _split3(x):
    b1 = x.astype(BF16)
    r1 = x - b1.astype(F32)
    b2 = r1.astype(BF16)
    b3 = (r1 - b2.astype(F32)).astype(BF16)
    return b1, b2, b3


def _hi_lo(x):
    hi = x.astype(BF16)
    return hi, (x - hi.astype(F32)).astype(BF16)


def _dot_split(a_parts, b_parts):
    ah, al = a_parts
    bh, bl = b_parts
    c = bh.shape[0]
    rhs = jnp.concatenate([jnp.concatenate([bh, bl], axis=1),
                           jnp.concatenate([bh, jnp.zeros_like(bh)], axis=1)], axis=0)
    r = _dot(jnp.concatenate([ah, al], axis=1), rhs)
    return r[:, :c] + r[:, c:]


def _unit_lower_inverse(lows, eye):
    c = lows[0].shape[0]
    n = range(len(lows))
    low_parts = [_hi_lo(low) for low in lows]
    pw = [_dot_split(low_parts[i], low_parts[i]) for i in n]
    t_inv = [eye - low for low in lows]
    for _ in range(int(math.log2(c)) - 2):
        pw_parts = [_hi_lo(p) for p in pw]
        t_parts = [_hi_lo(t) for t in t_inv]
        stacked = [tuple(jnp.concatenate([t, p], axis=0) for t, p in zip(t_parts[i], pw_parts[i])) for i in n]
        both = [_dot_split(stacked[i], pw_parts[i]) for i in n]
        t_inv = [t_inv[i] + both[i][:c] for i in n]
        pw = [both[i][c:] for i in n]
    return [t_inv[i] + _dot_split(_hi_lo(t_inv[i]), _hi_lo(pw[i])) for i in n]


def _gdn_body(q_ref, k_ref, v_ref, gate_ref, bg_ref, nw_ref, o_ref, s_scr):
    c = GDN_CHUNK

    @pl.when(pl.program_id(1) == 0)
    def _():
        s_scr[...] = jnp.zeros_like(s_scr)

    row = lax.broadcasted_iota(jnp.int32, (c, c), 0)
    col = lax.broadcasted_iota(jnp.int32, (c, c), 1)
    incl = col <= row
    strict = col < row
    eye = (col == row).astype(F32)
    tri = incl.astype(BF16)
    bg = bg_ref[...]
    gc_all = sum(_dot(tri, part) for part in _split3(bg))
    gc_rows = gc_all.T
    nw = nw_ref[...]
    heads = range(GDN_HEADS)
    sl = [slice(h * HEAD_COLS, (h + 1) * HEAD_COLS) for h in heads]
    state = [s_scr[h] for h in heads]
    q = [q_ref[:, sl[h]].astype(F32) for h in heads]
    k = [k_ref[:, sl[h]].astype(F32) for h in heads]
    v = [v_ref[:, sl[h]].astype(F32) for h in heads]
    beta = [bg[:, h:h + 1] for h in heads]
    gcb = [jnp.broadcast_to(gc_all[:, GDN_HEADS + h:GDN_HEADS + h + 1], (c, c)) for h in heads]
    gct = [jnp.broadcast_to(gc_rows[GDN_HEADS + h:GDN_HEADS + h + 1, :], (c, c)) for h in heads]
    decay = [jnp.exp(jnp.where(incl, gcb[h] - gct[h], NEG_INF)) for h in heads]
    eg = [jnp.exp(gcb[h]) for h in heads]
    g_last = [gcb[h][c - 1:c, :] for h in heads]
    kb = [k[h] * beta[h] for h in heads]
    kbf = [k[h].astype(BF16) for h in heads]
    low = [jnp.where(strict, _dot_nt(kb[h].astype(BF16), kbf[h]) * decay[h], 0.0) for h in heads]
    aqk = [jnp.where(incl, _dot_nt(q[h].astype(BF16), kbf[h]) * decay[h], 0.0).astype(BF16) for h in heads]
    tb = [t.astype(BF16) for t in _unit_lower_inverse(low, eye)]
    u = [_dot(tb[h], (v[h] * beta[h]).astype(BF16)) for h in heads]
    w = [_dot(tb[h], (kb[h] * eg[h]).astype(BF16)).astype(BF16) for h in heads]
    sb = [state[h].astype(BF16) for h in heads]
    vnb = [(u[h] - _dot(w[h], sb[h])).astype(BF16) for h in heads]
    o = [_dot((q[h] * eg[h]).astype(BF16), sb[h]) + _dot(aqk[h], vnb[h]) for h in heads]
    k_dec = [(k[h] * jnp.exp(g_last[h] - gcb[h])).T.astype(BF16) for h in heads]
    new_state = [state[h] * jnp.exp(g_last[h]) + _dot(k_dec[h], vnb[h]) for h in heads]
    for h in heads:
        s_scr[h] = new_state[h]
        o_ref[:, sl[h]] = (_rms(o[h], nw) * _silu(gate_ref[:, sl[h]].astype(F32))).astype(BF16)


def _gdn(q, k, v, gate, bg, norm_w, *, batch, seq):
    mix = GDN_HEADS * HEAD_COLS
    c = GDN_CHUNK
    per_seq = seq // c
    row = lambda n: pl.BlockSpec((c, n), lambda b, t: (b * per_seq + t, 0))
    return pl.pallas_call(
        _gdn_body,
        grid=(batch, per_seq),
        in_specs=[row(mix), row(mix), row(mix), row(mix), row(HEAD_COLS),
                  pl.BlockSpec((1, HEAD_COLS), lambda b, t: (0, 0))],
        out_specs=row(mix),
        out_shape=jax.ShapeDtypeStruct((batch * seq, mix), BF16),
        scratch_shapes=[pltpu.VMEM((GDN_HEADS, HEAD_COLS, HEAD_COLS), F32)],
        compiler_params=_cparams("parallel", "arbitrary"),
        name="gdn",
    )(q, k, v, gate, bg, norm_w.reshape(1, HEAD_COLS))


def _even_mixer(x, norm_w, w_in, diff_lambda, diff_subln, w_out, layer_idx, *, batch, seq):
    n_in = w_in.shape[1]
    q_cols = DIFF_HEADS * HEAD_COLS
    sb_q0 = 3 * q_cols
    col = jnp.arange(n_in)
    log2e = math.log2(math.e)
    col_scale = jnp.where(col < q_cols, DIFF_QK_DIM ** -0.5 * log2e,
                          jnp.where((col >= sb_q0) & (col < sb_q0 + SB_HEADS * HEAD_COLS), HEAD_COLS ** -0.5, 1.0))
    proj = _att_in(x, norm_w, w_in, col_scale.astype(F32).reshape(1, n_in)).reshape(batch, seq, n_in)
    lambda_init = 0.8 - 0.6 * math.exp(-0.3 * layer_idx)
    lp = diff_lambda.astype(F32)
    lam = jnp.exp(jnp.sum(lp[0] * lp[1])) - jnp.exp(jnp.sum(lp[2] * lp[3])) + lambda_init
    slopes2 = 2.0 ** (-8.0 * jnp.arange(1, DIFF_HEADS + 1, dtype=F32) / DIFF_HEADS) * log2e
    kpos, qpos = _alibi_lanes(slopes2, seq)
    oa = _diff_attn(proj, slopes2, lam.reshape(1), diff_subln.astype(F32), kpos, qpos, batch=batch, seq=seq,
                    out_scale=1.0 - lambda_init)
    osb = _sb_attn(proj, batch=batch, seq=seq)
    m = batch * seq
    return _proj_out(x, [oa.reshape(m, -1), osb.reshape(m, -1)], [w_out[:q_cols], w_out[q_cols:]])


def _odd_mixer(x, norm_w, w_in, conv_w, a_log, dt_bias, gdn_norm, w_out, *, batch, seq):
    q, k, v, gate, bg = _gdn_in(x, norm_w, w_in, conv_w, a_log, dt_bias, seq=seq)
    o = _gdn(q, k, v, gate, bg, gdn_norm.astype(F32), batch=batch, seq=seq)
    return _proj_out(x, [o], [w_out])


def kernel(x, ffn1_norm, ffn1_w_gate, ffn1_w_up, ffn1_w_down, mix_norm, att_w_in, diff_lambda, diff_subln, att_w_out, gdn_w_in, gdn_conv_w, gdn_a_log, gdn_dt_bias, gdn_norm, gdn_w_out, ffn2_norm, ffn2_w_gate, ffn2_w_up, ffn2_w_down, final_norm):
    batch, seq, d = x.shape
    depth = ffn1_norm.shape[0]
    xf = x.reshape(batch * seq, d)
    for layer in range(depth):
        xf = _ffn(xf, ffn1_norm[layer], ffn1_w_gate[layer], ffn1_w_up[layer], ffn1_w_down[layer])
        if layer % 2 == 0:
            e = layer // 2
            xf = _even_mixer(xf, mix_norm[layer], att_w_in[e], diff_lambda[e], diff_subln[e], att_w_out[e], layer,
                             batch=batch, seq=seq)
        else:
            o = layer // 2
            xf = _odd_mixer(xf, mix_norm[layer], gdn_w_in[o], gdn_conv_w[o], gdn_a_log[o], gdn_dt_bias[o],
                            gdn_norm[o], gdn_w_out[o], batch=batch, seq=seq)
        last = layer == depth - 1
        xf = _ffn(xf, ffn2_norm[layer], ffn2_w_gate[layer], ffn2_w_up[layer], ffn2_w_down[layer],
                  final_w=final_norm if last else None)
    return xf.reshape(batch, seq, d)
```

```python
import functools
import math

import jax
import jax.numpy as jnp
from jax import lax
from jax.experimental import pallas as pl
from jax.experimental.pallas import tpu as pltpu

F32 = jnp.float32
BF16 = jnp.bfloat16

D_FF = 2816
RMS_EPS = 1e-6
L2_EPS = 1e-6
NEG_INF = -1e30

MASK_CHUNK_LOG2 = 6
DIFF_HEADS = 4
DIFF_QK_DIM = 64
SB_HEADS = 4
HEAD_COLS = 128
GDN_HEADS = 8
GDN_CONV = 4
GDN_CHUNK = 128

V7X_VMEM_LIMIT_BYTES = 56 * 1024 * 1024
CONV_HALO = 16
CONV_LEAD = 8


def _cparams(*semantics):
    return pltpu.CompilerParams(dimension_semantics=semantics, vmem_limit_bytes=V7X_VMEM_LIMIT_BYTES)


def _resident(shape):
    zeros = (0,) * len(shape)
    return pl.BlockSpec(shape, lambda *_: zeros, pipeline_mode=pl.Buffered(1))


def _rms(x, w):
    ms = jnp.mean(x * x, axis=-1, keepdims=True)
    return x * lax.rsqrt(ms + RMS_EPS) * w


def _silu(x):
    h = 0.5 * x
    return h + h * jnp.tanh(h)


def _softplus(x):
    return jnp.maximum(x, 0.0) + jnp.log(1.0 + jnp.exp(-jnp.abs(x)))


def _dot(a, b):
    return jnp.dot(a, b, preferred_element_type=F32)


def _dot_nt(a, b):
    return lax.dot_general(a, b, (((1,), (1,)), ((), ())), preferred_element_type=F32)


def _ffn_body(x_ref, nw_ref, wg_ref, wu_ref, wd_ref, *rest, ff_chunk, final):
    if final:
        fw_ref, o_ref, a_ref = rest
    else:
        o_ref, a_ref = rest
    x = x_ref[...]
    h = _rms(x, nw_ref[...]).astype(BF16)
    for c in range(0, D_FF, ff_chunk):
        g = _dot(h, wg_ref[:, c:c + ff_chunk])
        u = _dot(h, wu_ref[:, c:c + ff_chunk])
        a_ref[:, c:c + ff_chunk] = (_silu(g) * u).astype(BF16)
    y = x + 0.5 * _dot(a_ref[...], wd_ref[...])
    if final:
        y = _rms(y, fw_ref[...])
    o_ref[...] = y


def _ffn(x, nw, wg, wu, wd, final_w=None, *, tm=512, ff_chunk=256):
    m, d = x.shape
    final = final_w is not None
    row = pl.BlockSpec((tm, d), lambda i: (i, 0))
    in_specs = [row, _resident((1, d)), _resident((d, D_FF)), _resident((d, D_FF)), _resident((D_FF, d))]
    args = [x, nw.reshape(1, d), wg.astype(BF16), wu.astype(BF16), wd.astype(BF16)]
    if final:
        in_specs.append(_resident((1, d)))
        args.append(final_w.reshape(1, d))
    return pl.pallas_call(
        functools.partial(_ffn_body, ff_chunk=ff_chunk, final=final),
        grid=(m // tm,),
        in_specs=in_specs,
        out_specs=row,
        out_shape=jax.ShapeDtypeStruct((m, d), F32),
        scratch_shapes=[pltpu.VMEM((tm, D_FF), BF16)],
        compiler_params=_cparams("parallel"),
        name="ffn",
    )(*args)


def _proj_out_body(*refs, n_lhs):
    x_ref = refs[0]
    o_ref = refs[-1]
    y = x_ref[...]
    for a_ref, w_ref in zip(refs[1:1 + n_lhs], refs[1 + n_lhs:1 + 2 * n_lhs]):
        y = y + _dot(a_ref[...], w_ref[...])
    o_ref[...] = y


def _proj_out(x, lhs, ws, *, tm=512):
    m, d = x.shape
    row = pl.BlockSpec((tm, d), lambda i: (i, 0))
    in_specs = [row]
    in_specs += [pl.BlockSpec((tm, a.shape[1]), lambda i: (i, 0)) for a in lhs]
    in_specs += [_resident(w.shape) for w in ws]
    return pl.pallas_call(
        functools.partial(_proj_out_body, n_lhs=len(lhs)),
        grid=(m // tm,),
        in_specs=in_specs,
        out_specs=row,
        out_shape=jax.ShapeDtypeStruct((m, d), F32),
        compiler_params=_cparams("parallel"),
        name="proj_out",
    )(x, *lhs, *[w.astype(BF16) for w in ws])


def _att_in_body(x_ref, nw_ref, w_ref, sc_ref, o_ref, *, n_chunk):
    h = _rms(x_ref[...], nw_ref[...]).astype(BF16)
    n = w_ref.shape[1]
    for c in range(0, n, n_chunk):
        p = _dot(h, w_ref[:, c:c + n_chunk])
        o_ref[:, c:c + n_chunk] = (p * sc_ref[:, c:c + n_chunk]).astype(BF16)


def _att_in(x, nw, w, col_scale, *, tm=512, n_chunk=512):
    m, d = x.shape
    n = w.shape[1]
    return pl.pallas_call(
        functools.partial(_att_in_body, n_chunk=n_chunk),
        grid=(m // tm,),
        in_specs=[pl.BlockSpec((tm, d), lambda i: (i, 0)), _resident((1, d)), _resident((d, n)),
                  _resident((1, n))],
        out_specs=pl.BlockSpec((tm, n), lambda i: (i, 0)),
        out_shape=jax.ShapeDtypeStruct((m, n), BF16),
        compiler_params=_cparams("parallel"),
        name="att_in",
    )(x, nw.reshape(1, d), w.astype(BF16), col_scale)


def _head_cols(h):
    return slice(h * HEAD_COLS, (h + 1) * HEAD_COLS)


def _store_v_transposed(v_ref, vt_scr, tk):
    for h in range(vt_scr.shape[0]):
        for j in range(vt_scr.shape[1]):
            vt_scr[h, j] = v_ref[j * tk:(j + 1) * tk, _head_cols(h)].astype(F32).T.astype(BF16)


def _diff_body(slope_ref, lam_ref, q_ref, k_ref, v_ref, kpos_ref, qpos_ref, sw_ref, o_ref, vt_scr, *, tq, span,
               out_scale):
    i = pl.program_id(1)
    lam = lam_ref[0]
    heads = range(DIFF_HEADS)
    chains = [(h, c) for h in heads for c in range(2)]

    @pl.when(i == 0)
    def _():
        _store_v_transposed(v_ref, vt_scr, tq)

    lane = lax.broadcasted_iota(jnp.int32, (tq, HEAD_COLS), 1)
    first = lane < DIFF_QK_DIM
    q_cat = []
    for h in heads:
        q = q_ref[:, _head_cols(h)]
        zero = jnp.zeros_like(q)
        qpos = jnp.broadcast_to(qpos_ref[h], q.shape)
        q_cat += [jnp.concatenate([jnp.where(first, q, zero), qpos], axis=1),
                  jnp.concatenate([jnp.where(first, zero, q), qpos], axis=1)]

    q_cat_t = [qc.astype(F32).T.astype(BF16) for qc in q_cat]

    def block(j, carry, fixups, span=1):
        n = range(len(chains))
        start = pl.multiple_of(j * tq, tq)
        kpos = kpos_ref[pl.ds(start, span * tq), :]
        k_cat = [jnp.concatenate([k_ref[pl.ds(start, span * tq), _head_cols(h)], kpos], axis=1) for h in heads]
        vt = [jnp.concatenate([vt_scr[h, j + b] for b in range(span)], axis=1) if span > 1 else vt_scr[h, j]
              for h in heads]
        m, l, acc = carry[0::3], carry[1::3], carry[2::3]
        s = [_dot(k_cat[h], q_cat_t[x]) for x, (h, _) in enumerate(chains)]
        if fixups is not None:
            s = [s[x] + fixups[h] for x, (h, _) in enumerate(chains)]
        m_new = [jnp.maximum(m[x], jnp.max(s[x], axis=0, keepdims=True)) for x in n]
        alpha = [jnp.exp2(m[x] - m_new[x]) for x in n]
        p = [jnp.exp2(s[x] - m_new[x]) for x in n]
        l = [alpha[x] * l[x] + jnp.sum(p[x], axis=0, keepdims=True) for x in n]
        pv = [_dot(vt[h], p[x].astype(BF16)) for x, (h, _) in enumerate(chains)]
        acc = [alpha[x] * acc[x] + pv[x] for x in n]
        return tuple(v for x in n for v in (m_new[x], l[x], acc[x]))

    init = (jnp.full((1, tq), NEG_INF, F32), jnp.zeros((1, tq), F32),
            jnp.zeros((HEAD_COLS, tq), F32)) * len(chains)
    carry = lax.fori_loop(0, i // span, lambda j, c: block(span * j, c, None, span), init)
    carry = lax.fori_loop(span * (i // span), i, lambda j, c: block(j, c, None), carry)
    kj = lax.broadcasted_iota(jnp.int32, (tq, tq), 0)
    qi = lax.broadcasted_iota(jnp.int32, (tq, tq), 1)
    ahead = jnp.where(kj > qi, (qi - kj).astype(F32), 0.0)
    allowed = (kj >> MASK_CHUNK_LOG2) <= (qi >> MASK_CHUNK_LOG2)
    fixups = [jnp.where(allowed, (2.0 * slope_ref[h]) * ahead, NEG_INF) for h in heads]
    out = block(i, carry, fixups)
    sw = sw_ref[...]
    for h in heads:
        _, l0, a0, _, l1, a1 = out[6 * h:6 * h + 6]
        o = (a0 * (1.0 / l0) - (lam / l1) * a1).T
        o_ref[:, _head_cols(h)] = (_rms(o, sw) * out_scale).astype(BF16)


def _diff_attn(proj, slopes2, lam, subln, kpos, qpos, *, batch, seq, out_scale, tq=256, span=2):
    smem = pl.BlockSpec(memory_space=pltpu.SMEM)
    width = DIFF_HEADS * HEAD_COLS
    kv = lambda blk: pl.BlockSpec((None, seq, width), lambda b, i: (b, 0, blk))
    return pl.pallas_call(
        functools.partial(_diff_body, tq=tq, span=span, out_scale=out_scale),
        grid=(batch, seq // tq),
        in_specs=[smem, smem,
                  pl.BlockSpec((None, tq, width), lambda b, i: (b, i, 0)),
                  kv(1), kv(2),
                  pl.BlockSpec((seq, HEAD_COLS), lambda b, i: (0, 0)),
                  pl.BlockSpec((DIFF_HEADS, 1, HEAD_COLS), lambda b, i: (0, 0, 0)),
                  pl.BlockSpec((1, HEAD_COLS), lambda b, i: (0, 0))],
        out_specs=pl.BlockSpec((None, tq, width), lambda b, i: (b, i, 0)),
        out_shape=jax.ShapeDtypeStruct((batch, seq, width), BF16),
        scratch_shapes=[pltpu.VMEM((DIFF_HEADS, seq // tq, HEAD_COLS, tq), BF16)],
        compiler_params=_cparams("parallel", "arbitrary"),
        name="diff_attn",
    )(slopes2, lam, proj, proj, proj, kpos, qpos, subln.reshape(1, HEAD_COLS))


def _alibi_lanes(slopes2, seq):
    pos = jnp.arange(seq)
    a = (pos >> MASK_CHUNK_LOG2).astype(F32)
    b = (pos & ((1 << MASK_CHUNK_LOG2) - 1)).astype(F32)
    kpos = jnp.zeros((seq, HEAD_COLS), F32).at[:, 0:3].set(a[:, None]).at[:, 3:6].set(b[:, None]).astype(BF16)

    def split3(x):
        x1 = x.astype(BF16).astype(F32)
        x2 = (x - x1).astype(BF16).astype(F32)
        return jnp.stack([x1, x2, (x - x1 - x2).astype(BF16).astype(F32)], axis=-1)

    mult = jnp.concatenate([split3(slopes2 * (1 << MASK_CHUNK_LOG2)), split3(slopes2)], axis=-1)
    qpos = jnp.zeros((slopes2.shape[0], 1, HEAD_COLS), F32).at[:, 0, 0:6].set(mult).astype(BF16)
    return kpos, qpos


SB_GROUP = 128


def _sb_body(q_ref, k_ref, v_ref, o_ref, vt_scr, *, tq, span):
    i = pl.program_id(1)
    heads = range(SB_HEADS)

    @pl.when(i == 0)
    def _():
        _store_v_transposed(v_ref, vt_scr, tq)

    q = [q_ref[:, _head_cols(h)] for h in heads]
    gs = lax.broadcasted_iota(jnp.int32, (SB_GROUP, SB_GROUP), 0)
    gj = lax.broadcasted_iota(jnp.int32, (SB_GROUP, SB_GROUP), 1)
    later = (gj > gs).astype(BF16)
    later2 = jnp.concatenate([later, later], axis=1)

    def block(j, carry, earlier, span=1):
        tail, acc = list(carry[0::2]), carry[1::2]
        start = pl.multiple_of(j * tq, tq)
        groups = [slice(g * SB_GROUP, (g + 1) * SB_GROUP) for g in range(span * tq // SB_GROUP)]
        z = [_dot_nt(k_ref[pl.ds(start, span * tq), _head_cols(h)], q[h]) for h in heads]
        sp = [jnp.log2(1.0 + jnp.exp2(-jnp.abs(z[h]))) for h in heads]
        log_beta = [jnp.minimum(z[h], 0.0) - sp[h] for h in heads]
        neg_keep = [jnp.maximum(z[h], 0.0) + sp[h] for h in heads]
        if earlier is not None:
            neg_keep = [jnp.where(earlier, neg_keep[h], 0.0) for h in heads]
        after = [[None] * len(groups) for _ in heads]
        for g in reversed(range(len(groups))):
            x = [neg_keep[h][groups[g], :] for h in heads]
            parts = [_hi_lo(x[h]) for h in heads]
            within = [_dot(later2, jnp.concatenate(parts[h], axis=0)) for h in heads]
            for h in heads:
                after[h][g] = within[h] + tail[h]
                tail[h] = tail[h] + jnp.sum(x[h], axis=0, keepdims=True)
        a = [jnp.exp2(log_beta[h] - jnp.concatenate(after[h], axis=0)) for h in heads]
        if earlier is not None:
            a = [jnp.where(earlier, a[h], 0.0) for h in heads]
        vt = [jnp.concatenate([vt_scr[h, j + b] for b in range(span)], axis=1) if span > 1 else vt_scr[h, j]
              for h in heads]
        acc = [acc[h] + _dot(vt[h], a[h].astype(BF16)) for h in heads]
        return tuple(v for h in heads for v in (tail[h], acc[h]))

    kj = lax.broadcasted_iota(jnp.int32, (tq, tq), 0)
    qi = lax.broadcasted_iota(jnp.int32, (tq, tq), 1)
    init = (jnp.zeros((1, tq), F32), jnp.zeros((HEAD_COLS, tq), F32)) * SB_HEADS
    carry = block(i, init, kj < qi)
    odd = i % span
    carry = lax.fori_loop(0, odd, lambda t, c: block(i - 1 - t, c, None), carry)
    out = lax.fori_loop(0, i // span, lambda t, c: block(i - odd - span * (t + 1), c, None, span), carry)
    for h in heads:
        o_ref[:, _head_cols(h)] = out[2 * h + 1].T.astype(BF16)


def _sb_attn(proj, *, batch, seq, tq=256, span=2):
    width = SB_HEADS * HEAD_COLS
    kv = lambda blk: pl.BlockSpec((None, seq, width), lambda b, i: (b, 0, blk))
    return pl.pallas_call(
        functools.partial(_sb_body, tq=tq, span=span),
        grid=(batch, seq // tq),
        in_specs=[pl.BlockSpec((None, tq, width), lambda b, i: (b, i, 3)), kv(4), kv(5)],
        out_specs=pl.BlockSpec((None, tq, width), lambda b, i: (b, i, 0)),
        out_shape=jax.ShapeDtypeStruct((batch, seq, width), BF16),
        scratch_shapes=[pltpu.VMEM((SB_HEADS, seq // tq, HEAD_COLS, tq), BF16)],
        compiler_params=_cparams("parallel", "arbitrary"),
        name="sb_attn",
    )(proj, proj, proj)


def _gdn_in_body(x_ref, xp_ref, nw_ref, wqkv_ref, wg_ref, wba_ref, cw_ref, alog_ref, dt_ref,
                 q_ref, k_ref, v_ref, gate_ref, bg_ref, p_scr, *, tm, seq, n_chunk):
    i = pl.program_id(0)
    nw = nw_ref[...]
    h = _rms(x_ref[...], nw).astype(BF16)
    seq_start = (i * tm) % seq == 0
    hp = jnp.where(seq_start, 0.0, _rms(xp_ref[...], nw)).astype(BF16)
    mix = GDN_HEADS * HEAD_COLS
    outs = (q_ref, k_ref, v_ref)
    q_scale = HEAD_COLS ** -0.5
    for c in range(0, 3 * mix, n_chunk):
        p_scr[0:CONV_HALO, :] = _dot(hp, wqkv_ref[:, c:c + n_chunk])
        p_scr[CONV_HALO:CONV_HALO + tm, :] = _dot(h, wqkv_ref[:, c:c + n_chunk])
        for s in range(0, n_chunk, HEAD_COLS):
            col = c + s
            win = p_scr[CONV_HALO - CONV_LEAD:CONV_HALO + tm, s:s + HEAD_COLS]
            y = None
            for tap in range(GDN_CONV):
                delay = GDN_CONV - 1 - tap
                shifted = pltpu.roll(win, delay, axis=0) if delay else win
                term = shifted[CONV_LEAD:, :] * cw_ref[tap:tap + 1, col:col + HEAD_COLS]
                y = term if y is None else y + term
            y = _silu(y)
            which, local = divmod(col, mix)
            if which < 2:
                y = y * lax.rsqrt(jnp.sum(y * y, axis=-1, keepdims=True) + L2_EPS)
            if which == 0:
                y = y * q_scale
            outs[which][:, local:local + HEAD_COLS] = y.astype(BF16)
    for c in range(0, mix, n_chunk):
        gate_ref[:, c:c + n_chunk] = _dot(h, wg_ref[:, c:c + n_chunk]).astype(BF16)
    ba = _dot(h, wba_ref[...])
    lane = lax.broadcasted_iota(jnp.int32, ba.shape, 1)
    beta = jax.nn.sigmoid(ba)
    g = -jnp.exp(alog_ref[...]) * _softplus(ba + dt_ref[...])
    bg_ref[...] = jnp.where(lane < GDN_HEADS, beta, jnp.where(lane < 2 * GDN_HEADS, g, 0.0))


def _gdn_in(x, nw, w_in, conv_w, a_log, dt_bias, *, seq, tm=512, n_chunk=512):
    m, d = x.shape
    mix = GDN_HEADS * HEAD_COLS
    w_qkv = w_in[:, :3 * mix].astype(BF16)
    w_gate = w_in[:, 3 * mix:4 * mix].astype(BF16)
    pad = HEAD_COLS - 2 * GDN_HEADS
    w_ba = jnp.pad(w_in[:, 4 * mix:], ((0, 0), (0, pad))).astype(BF16)
    lanes = lambda v: jnp.pad(v.astype(F32), (GDN_HEADS, pad)).reshape(1, HEAD_COLS)
    row = lambda n: pl.BlockSpec((tm, n), lambda i: (i, 0))
    halo_blocks = tm // CONV_HALO
    act = jax.ShapeDtypeStruct((m, mix), BF16)
    return pl.pallas_call(
        functools.partial(_gdn_in_body, tm=tm, seq=seq, n_chunk=n_chunk),
        grid=(m // tm,),
        in_specs=[row(d),
                  pl.BlockSpec((CONV_HALO, d), lambda i: (jnp.maximum(i * halo_blocks - 1, 0), 0)),
                  _resident((1, d)), _resident((d, 3 * mix)), _resident((d, mix)), _resident((d, HEAD_COLS)),
                  _resident((GDN_CONV, 3 * mix)), _resident((1, HEAD_COLS)), _resident((1, HEAD_COLS))],
        out_specs=[row(mix), row(mix), row(mix), row(mix), row(HEAD_COLS)],
        out_shape=[act, act, act, act, jax.ShapeDtypeStruct((m, HEAD_COLS), F32)],
        scratch_shapes=[pltpu.VMEM((CONV_HALO + tm, n_chunk), F32)],
        compiler_params=_cparams("parallel"),
        name="gdn_in",
    )(x, x, nw.reshape(1, d), w_qkv, w_gate, w_ba, conv_w.astype(F32), lanes(a_log), lanes(dt_bias))


def _split3(x):
    b1 = x.astype(BF16)
    r1 = x - b1.astype(F32)
    b2 = r1.astype(BF16)
    b3 = (r1 - b2.astype(F32)).astype(BF16)
    return b1, b2, b3


def _hi_lo(x):
    hi = x.astype(BF16)
    return hi, (x - hi.astype(F32)).astype(BF16)


def _dot_split(a_parts, b_parts):
    ah, al = a_parts
    bh, bl = b_parts
    c = bh.shape[0]
    rhs = jnp.concatenate([jnp.concatenate([bh, bl], axis=1),
                           jnp.concatenate([bh, jnp.zeros_like(bh)], axis=1)], axis=0)
    r = _dot(jnp.concatenate([ah, al], axis=1), rhs)
    return r[:, :c] + r[:, c:]


def _unit_lower_inverse(lows, eye):
    c = lows[0].shape[0]
    n = range(len(lows))
    low_parts = [_hi_lo(low) for low in lows]
    pw = [_dot_split(low_parts[i], low_parts[i]) for i in n]
    t_inv = [eye - low for low in lows]
    for _ in range(int(math.log2(c)) - 2):
        pw_parts = [_hi_lo(p) for p in pw]
        t_parts = [_hi_lo(t) for t in t_inv]
        stacked = [tuple(jnp.concatenate([t, p], axis=0) for t, p in zip(t_parts[i], pw_parts[i])) for i in n]
        both = [_dot_split(stacked[i], pw_parts[i]) for i in n]
        t_inv = [t_inv[i] + both[i][:c] for i in n]
        pw = [both[i][c:] for i in n]
    return [t_inv[i] + _dot_split(_hi_lo(t_inv[i]), _hi_lo(pw[i])) for i in n]


def _gdn_body(q_ref, k_ref, v_ref, gate_ref, bg_ref, nw_ref, o_ref, s_scr):
    c = GDN_CHUNK

    @pl.when(pl.program_id(1) == 0)
    def _():
        s_scr[...] = jnp.zeros_like(s_scr)

    row = lax.broadcasted_iota(jnp.int32, (c, c), 0)
    col = lax.broadcasted_iota(jnp.int32, (c, c), 1)
    incl = col <= row
    strict = col < row
    eye = (col == row).astype(F32)
    tri = incl.astype(BF16)
    bg = bg_ref[...]
    gc_all = sum(_dot(tri, part) for part in _split3(bg))
    gc_rows = gc_all.T
    nw = nw_ref[...]
    heads = range(GDN_HEADS)
    sl = [slice(h * HEAD_COLS, (h + 1) * HEAD_COLS) for h in heads]
    state = [s_scr[h] for h in heads]
    q = [q_ref[:, sl[h]].astype(F32) for h in heads]
    k = [k_ref[:, sl[h]].astype(F32) for h in heads]
    v = [v_ref[:, sl[h]].astype(F32) for h in heads]
    beta = [bg[:, h:h + 1] for h in heads]
    gcb = [jnp.broadcast_to(gc_all[:, GDN_HEADS + h:GDN_HEADS + h + 1], (c, c)) for h in heads]
    gct = [jnp.broadcast_to(gc_rows[GDN_HEADS + h:GDN_HEADS + h + 1, :], (c, c)) for h in heads]
    decay = [jnp.exp(jnp.where(incl, gcb[h] - gct[h], NEG_INF)) for h in heads]
    eg = [jnp.exp(gcb[h]) for h in heads]
    g_last = [gcb[h][c - 1:c, :] for h in heads]
    kb = [k[h] * beta[h] for h in heads]
    kbf = [k[h].astype(BF16) for h in heads]
    low = [jnp.where(strict, _dot_nt(kb[h].astype(BF16), kbf[h]) * decay[h], 0.0) for h in heads]
    aqk = [jnp.where(incl, _dot_nt(q[h].astype(BF16), kbf[h]) * decay[h], 0.0).astype(BF16) for h in heads]
    tb = [t.astype(BF16) for t in _unit_lower_inverse(low, eye)]
    u = [_dot(tb[h], (v[h] * beta[h]).astype(BF16)) for h in heads]
    w = [_dot(tb[h], (kb[h] * eg[h]).astype(BF16)).astype(BF16) for h in heads]
    sb = [state[h].astype(BF16) for h in heads]
    vnb = [(u[h] - _dot(w[h], sb[h])).astype(BF16) for h in heads]
    o = [_dot((q[h] * eg[h]).astype(BF16), sb[h]) + _dot(aqk[h], vnb[h]) for h in heads]
    k_dec = [(k[h] * jnp.exp(g_last[h] - gcb[h])).T.astype(BF16) for h in heads]
    new_state = [state[h] * jnp.exp(g_last[h]) + _dot(k_dec[h], vnb[h]) for h in heads]
    for h in heads:
        s_scr[h] = new_state[h]
        o_ref[:, sl[h]] = (_rms(o[h], nw) * _silu(gate_ref[:, sl[h]].astype(F32))).astype(BF16)


def _gdn(q, k, v, gate, bg, norm_w, *, batch, seq):
    mix = GDN_HEADS * HEAD_COLS
    c = GDN_CHUNK
    per_seq = seq // c
    row = lambda n: pl.BlockSpec((c, n), lambda b, t: (b * per_seq + t, 0))
    return pl.pallas_call(
        _gdn_body,
        grid=(batch, per_seq),
        in_specs=[row(mix), row(mix), row(mix), row(mix), row(HEAD_COLS),
                  pl.BlockSpec((1, HEAD_COLS), lambda b, t: (0, 0))],
        out_specs=row(mix),
        out_shape=jax.ShapeDtypeStruct((batch * seq, mix), BF16),
        scratch_shapes=[pltpu.VMEM((GDN_HEADS, HEAD_COLS, HEAD_COLS), F32)],
        compiler_params=_cparams("parallel", "arbitrary"),
        name="gdn",
    )(q, k, v, gate, bg, norm_w.reshape(1, HEAD_COLS))


def _even_mixer(x, norm_w, w_in, diff_lambda, diff_subln, w_out, layer_idx, *, batch, seq):
    n_in = w_in.shape[1]
    q_cols = DIFF_HEADS * HEAD_COLS
    sb_q0 = 3 * q_cols
    col = jnp.arange(n_in)
    log2e = math.log2(math.e)
    col_scale = jnp.where(col < q_cols, DIFF_QK_DIM ** -0.5 * log2e,
                          jnp.where((col >= sb_q0) & (col < sb_q0 + SB_HEADS * HEAD_COLS),
                                    HEAD_COLS ** -0.5 * log2e, 1.0))
    proj = _att_in(x, norm_w, w_in, col_scale.astype(F32).reshape(1, n_in)).reshape(batch, seq, n_in)
    lambda_init = 0.8 - 0.6 * math.exp(-0.3 * layer_idx)
    lp = diff_lambda.astype(F32)
    lam = jnp.exp(jnp.sum(lp[0] * lp[1])) - jnp.exp(jnp.sum(lp[2] * lp[3])) + lambda_init
    slopes2 = 2.0 ** (-8.0 * jnp.arange(1, DIFF_HEADS + 1, dtype=F32) / DIFF_HEADS) * log2e
    kpos, qpos = _alibi_lanes(slopes2, seq)
    oa = _diff_attn(proj, slopes2, lam.reshape(1), diff_subln.astype(F32), kpos, qpos, batch=batch, seq=seq,
                    out_scale=1.0 - lambda_init)
    osb = _sb_attn(proj, batch=batch, seq=seq)
    m = batch * seq
    return _proj_out(x, [oa.reshape(m, -1), osb.reshape(m, -1)], [w_out[:q_cols], w_out[q_cols:]])


def _odd_mixer(x, norm_w, w_in, conv_w, a_log, dt_bias, gdn_norm, w_out, *, batch, seq):
    q, k, v, gate, bg = _gdn_in(x, norm_w, w_in, conv_w, a_log, dt_bias, seq=seq)
    o = _gdn(q, k, v, gate, bg, gdn_norm.astype(F32), batch=batch, seq=seq)
    return _proj_out(x, [o], [w_out])


def kernel(x, ffn1_norm, ffn1_w_gate, ffn1_w_up, ffn1_w_down, mix_norm, att_w_in, diff_lambda, diff_subln, att_w_out, gdn_w_in, gdn_conv_w, gdn_a_log, gdn_dt_bias, gdn_norm, gdn_w_out, ffn2_norm, ffn2_w_gate, ffn2_w_up, ffn2_w_down, final_norm):
    batch, seq, d = x.shape
    depth = ffn1_norm.shape[0]
    xf = x.reshape(batch * seq, d)
    for layer in range(depth):
        xf = _ffn(xf, ffn1_norm[layer], ffn1_w_gate[layer], ffn1_w_up[layer], ffn1_w_down[layer])
        if layer % 2 == 0:
            e = layer // 2
            xf = _even_mixer(xf, mix_norm[layer], att_w_in[e], diff_lambda[e], diff_subln[e], att_w_out[e], layer,
                             batch=batch, seq=seq)
        else:
            o = layer // 2
            xf = _odd_mixer(xf, mix_norm[layer], gdn_w_in[o], gdn_conv_w[o], gdn_a_log[o], gdn_dt_bias[o],
                            gdn_norm[o], gdn_w_out[o], batch=batch, seq=seq)
        last = layer == depth - 1
        xf = _ffn(xf, ffn2_norm[layer], ffn2_w_gate[layer], ffn2_w_up[layer], ffn2_w_down[layer],
                  final_w=final_norm if last else None)
    return xf.reshape(batch, seq, d)
```

```python
import functools
import math

import jax
import jax.numpy as jnp
from jax import lax
from jax.experimental import pallas as pl
from jax.experimental.pallas import tpu as pltpu

F32 = jnp.float32
BF16 = jnp.bfloat16

D_FF = 2816
RMS_EPS = 1e-6
L2_EPS = 1e-6
NEG_INF = -1e30

MASK_CHUNK_LOG2 = 6
DIFF_HEADS = 4
DIFF_QK_DIM = 64
SB_HEADS = 4
HEAD_COLS = 128
GDN_HEADS = 8
GDN_CONV = 4
GDN_CHUNK = 128

V7X_VMEM_LIMIT_BYTES = 56 * 1024 * 1024
CONV_HALO = 16
CONV_LEAD = 8


def _cparams(*semantics):
    return pltpu.CompilerParams(dimension_semantics=semantics, vmem_limit_bytes=V7X_VMEM_LIMIT_BYTES)


def _resident(shape):
    zeros = (0,) * len(shape)
    return pl.BlockSpec(shape, lambda *_: zeros, pipeline_mode=pl.Buffered(1))


def _rms(x, w):
    ms = jnp.mean(x * x, axis=-1, keepdims=True)
    return x * lax.rsqrt(ms + RMS_EPS) * w


def _silu(x):
    h = 0.5 * x
    return h + h * jnp.tanh(h)


def _softplus(x):
    return jnp.maximum(x, 0.0) + jnp.log(1.0 + jnp.exp(-jnp.abs(x)))


def _dot(a, b):
    return jnp.dot(a, b, preferred_element_type=F32)


def _dot_nt(a, b):
    return lax.dot_general(a, b, (((1,), (1,)), ((), ())), preferred_element_type=F32)


def _ffn_body(*refs, ff_chunk, n_mix, final):
    x_ref, nw_ref, wg_ref, wu_ref, wd_ref = refs[:5]
    mix_refs = refs[5:5 + n_mix]
    wout_refs = refs[5 + n_mix:5 + 2 * n_mix]
    rest = refs[5 + 2 * n_mix:]
    if final:
        fw_ref, o_ref, a_ref = rest
    else:
        o_ref, a_ref = rest
    x = x_ref[...]
    for mix_ref, wout_ref in zip(mix_refs, wout_refs):
        x = x + _dot(mix_ref[...], wout_ref[...])
    h = _rms(x, nw_ref[...]).astype(BF16)
    for c in range(0, D_FF, ff_chunk):
        g = _dot(h, wg_ref[:, c:c + ff_chunk].astype(BF16))
        u = _dot(h, wu_ref[:, c:c + ff_chunk].astype(BF16))
        a_ref[:, c:c + ff_chunk] = (_silu(g) * u).astype(BF16)
    y = x + 0.5 * _dot(a_ref[...], wd_ref[...].astype(BF16))
    if final:
        y = _rms(y, fw_ref[...])
    o_ref[...] = y


def _ffn(x, nw, wg, wu, wd, final_w=None, mix=(), w_out=(), *, tm=512, ff_chunk=256):
    m, d = x.shape
    final = final_w is not None
    row = pl.BlockSpec((tm, d), lambda i: (i, 0))
    in_specs = [row, _resident((1, d)), _resident((d, D_FF)), _resident((d, D_FF)), _resident((D_FF, d))]
    in_specs += [pl.BlockSpec((tm, a.shape[1]), lambda i: (i, 0)) for a in mix]
    in_specs += [_resident(w.shape) for w in w_out]
    args = [x, nw.reshape(1, d), wg, wu, wd, *mix, *[w.astype(BF16) for w in w_out]]
    if final:
        in_specs.append(_resident((1, d)))
        args.append(final_w.reshape(1, d))
    return pl.pallas_call(
        functools.partial(_ffn_body, ff_chunk=ff_chunk, n_mix=len(mix), final=final),
        grid=(m // tm,),
        in_specs=in_specs,
        out_specs=row,
        out_shape=jax.ShapeDtypeStruct((m, d), F32),
        scratch_shapes=[pltpu.VMEM((tm, D_FF), BF16)],
        compiler_params=_cparams("parallel"),
        name="ffn",
    )(*args)


def _att_in_body(x_ref, nw_ref, w_ref, sc_ref, o_ref, *, n_chunk):
    h = _rms(x_ref[...], nw_ref[...]).astype(BF16)
    n = w_ref.shape[1]
    for c in range(0, n, n_chunk):
        p = _dot(h, w_ref[:, c:c + n_chunk].astype(BF16))
        o_ref[:, c:c + n_chunk] = (p * sc_ref[:, c:c + n_chunk]).astype(BF16)


def _att_in(x, nw, w, col_scale, *, tm=512, n_chunk=512):
    m, d = x.shape
    n = w.shape[1]
    return pl.pallas_call(
        functools.partial(_att_in_body, n_chunk=n_chunk),
        grid=(m // tm,),
        in_specs=[pl.BlockSpec((tm, d), lambda i: (i, 0)), _resident((1, d)), _resident((d, n)),
                  _resident((1, n))],
        out_specs=pl.BlockSpec((tm, n), lambda i: (i, 0)),
        out_shape=jax.ShapeDtypeStruct((m, n), BF16),
        compiler_params=_cparams("parallel"),
        name="att_in",
    )(x, nw.reshape(1, d), w, col_scale)


def _head_cols(h):
    return slice(h * HEAD_COLS, (h + 1) * HEAD_COLS)


def _store_v_transposed(v_ref, vt_scr, tk):
    for h in range(vt_scr.shape[0]):
        for j in range(vt_scr.shape[1]):
            vt_scr[h, j] = v_ref[j * tk:(j + 1) * tk, _head_cols(h)].astype(F32).T.astype(BF16)


def _diff_body(slope_ref, lam_ref, q_ref, k_ref, v_ref, kpos_ref, qpos_ref, sw_ref, o_ref, vt_scr, *, tq, span,
               out_scale):
    i = pl.program_id(1)
    lam = lam_ref[0]
    heads = range(DIFF_HEADS)
    chains = [(h, c) for h in heads for c in range(2)]

    @pl.when(i == 0)
    def _():
        _store_v_transposed(v_ref, vt_scr, tq)

    lane = lax.broadcasted_iota(jnp.int32, (tq, HEAD_COLS), 1)
    first = lane < DIFF_QK_DIM
    q_cat = []
    for h in heads:
        q = q_ref[:, _head_cols(h)]
        zero = jnp.zeros_like(q)
        qpos = jnp.broadcast_to(qpos_ref[h], q.shape)
        q_cat += [jnp.concatenate([jnp.where(first, q, zero), qpos], axis=1),
                  jnp.concatenate([jnp.where(first, zero, q), qpos], axis=1)]

    q_cat_t = [qc.astype(F32).T.astype(BF16) for qc in q_cat]

    def block(j, carry, fixups, span=1):
        n = range(len(chains))
        start = pl.multiple_of(j * tq, tq)
        kpos = kpos_ref[pl.ds(start, span * tq), :]
        k_cat = [jnp.concatenate([k_ref[pl.ds(start, span * tq), _head_cols(h)], kpos], axis=1) for h in heads]
        vt = [jnp.concatenate([vt_scr[h, j + b] for b in range(span)], axis=1) if span > 1 else vt_scr[h, j]
              for h in heads]
        m, l, acc = carry[0::3], carry[1::3], carry[2::3]
        s = [_dot(k_cat[h], q_cat_t[x]) for x, (h, _) in enumerate(chains)]
        if fixups is not None:
            s = [s[x] + fixups[h] for x, (h, _) in enumerate(chains)]
        m_new = [jnp.maximum(m[x], jnp.max(s[x], axis=0, keepdims=True)) for x in n]
        alpha = [jnp.exp2(m[x] - m_new[x]) for x in n]
        p = [jnp.exp2(s[x] - m_new[x]) for x in n]
        l = [alpha[x] * l[x] + jnp.sum(p[x], axis=0, keepdims=True) for x in n]
        pv = [_dot(vt[h], p[x].astype(BF16)) for x, (h, _) in enumerate(chains)]
        acc = [alpha[x] * acc[x] + pv[x] for x in n]
        return tuple(v for x in n for v in (m_new[x], l[x], acc[x]))

    init = (jnp.full((1, tq), NEG_INF, F32), jnp.zeros((1, tq), F32),
            jnp.zeros((HEAD_COLS, tq), F32)) * len(chains)
    carry = lax.fori_loop(0, i // span, lambda j, c: block(span * j, c, None, span), init)
    carry = lax.fori_loop(span * (i // span), i, lambda j, c: block(j, c, None), carry)
    kj = lax.broadcasted_iota(jnp.int32, (tq, tq), 0)
    qi = lax.broadcasted_iota(jnp.int32, (tq, tq), 1)
    ahead = jnp.where(kj > qi, (qi - kj).astype(F32), 0.0)
    allowed = (kj >> MASK_CHUNK_LOG2) <= (qi >> MASK_CHUNK_LOG2)
    fixups = [jnp.where(allowed, (2.0 * slope_ref[h]) * ahead, NEG_INF) for h in heads]
    out = block(i, carry, fixups)
    sw = sw_ref[...]
    for h in heads:
        _, l0, a0, _, l1, a1 = out[6 * h:6 * h + 6]
        o = (a0 * (1.0 / l0) - (lam / l1) * a1).T
        o_ref[:, _head_cols(h)] = (_rms(o, sw) * out_scale).astype(BF16)


def _diff_attn(proj, slopes2, lam, subln, kpos, qpos, *, batch, seq, out_scale, tq=256, span=2):
    smem = pl.BlockSpec(memory_space=pltpu.SMEM)
    width = DIFF_HEADS * HEAD_COLS
    kv = lambda blk: pl.BlockSpec((None, seq, width), lambda b, i: (b, 0, blk))
    return pl.pallas_call(
        functools.partial(_diff_body, tq=tq, span=span, out_scale=out_scale),
        grid=(batch, seq // tq),
        in_specs=[smem, smem,
                  pl.BlockSpec((None, tq, width), lambda b, i: (b, i, 0)),
                  kv(1), kv(2),
                  pl.BlockSpec((seq, HEAD_COLS), lambda b, i: (0, 0)),
                  pl.BlockSpec((DIFF_HEADS, 1, HEAD_COLS), lambda b, i: (0, 0, 0)),
                  pl.BlockSpec((1, HEAD_COLS), lambda b, i: (0, 0))],
        out_specs=pl.BlockSpec((None, tq, width), lambda b, i: (b, i, 0)),
        out_shape=jax.ShapeDtypeStruct((batch, seq, width), BF16),
        scratch_shapes=[pltpu.VMEM((DIFF_HEADS, seq // tq, HEAD_COLS, tq), BF16)],
        compiler_params=_cparams("parallel", "arbitrary"),
        name="diff_attn",
    )(slopes2, lam, proj, proj, proj, kpos, qpos, subln.reshape(1, HEAD_COLS))


def _alibi_lanes(slopes2, seq):
    pos = jnp.arange(seq)
    a = (pos >> MASK_CHUNK_LOG2).astype(F32)
    b = (pos & ((1 << MASK_CHUNK_LOG2) - 1)).astype(F32)
    kpos = jnp.zeros((seq, HEAD_COLS), F32).at[:, 0:3].set(a[:, None]).at[:, 3:6].set(b[:, None]).astype(BF16)

    def split3(x):
        x1 = x.astype(BF16).astype(F32)
        x2 = (x - x1).astype(BF16).astype(F32)
        return jnp.stack([x1, x2, (x - x1 - x2).astype(BF16).astype(F32)], axis=-1)

    mult = jnp.concatenate([split3(slopes2 * (1 << MASK_CHUNK_LOG2)), split3(slopes2)], axis=-1)
    qpos = jnp.zeros((slopes2.shape[0], 1, HEAD_COLS), F32).at[:, 0, 0:6].set(mult).astype(BF16)
    return kpos, qpos


SB_GROUP = 128


def _sb_body(q_ref, k_ref, v_ref, o_ref, vt_scr, *, tq, span):
    i = pl.program_id(1)
    heads = range(SB_HEADS)

    @pl.when(i == 0)
    def _():
        _store_v_transposed(v_ref, vt_scr, tq)

    q = [q_ref[:, _head_cols(h)] for h in heads]
    gs = lax.broadcasted_iota(jnp.int32, (SB_GROUP, SB_GROUP), 0)
    gj = lax.broadcasted_iota(jnp.int32, (SB_GROUP, SB_GROUP), 1)
    later = (gj > gs).astype(BF16)
    later2 = jnp.concatenate([later, later], axis=1)

    def block(j, carry, earlier, span=1):
        tail, acc = list(carry[0::2]), carry[1::2]
        start = pl.multiple_of(j * tq, tq)
        groups = [slice(g * SB_GROUP, (g + 1) * SB_GROUP) for g in range(span * tq // SB_GROUP)]
        z = [_dot_nt(k_ref[pl.ds(start, span * tq), _head_cols(h)], q[h]) for h in heads]
        sp = [jnp.log2(1.0 + jnp.exp2(-jnp.abs(z[h]))) for h in heads]
        log_beta = [jnp.minimum(z[h], 0.0) - sp[h] for h in heads]
        neg_keep = [jnp.maximum(z[h], 0.0) + sp[h] for h in heads]
        if earlier is not None:
            neg_keep = [jnp.where(earlier, neg_keep[h], 0.0) for h in heads]
        after = [[None] * len(groups) for _ in heads]
        for g in reversed(range(len(groups))):
            x = [neg_keep[h][groups[g], :] for h in heads]
            parts = [_hi_lo(x[h]) for h in heads]
            within = [_dot(later2, jnp.concatenate(parts[h], axis=0)) for h in heads]
            for h in heads:
                after[h][g] = within[h] + tail[h]
                tail[h] = tail[h] + jnp.sum(x[h], axis=0, keepdims=True)
        a = [jnp.exp2(log_beta[h] - jnp.concatenate(after[h], axis=0)) for h in heads]
        if earlier is not None:
            a = [jnp.where(earlier, a[h], 0.0) for h in heads]
        vt = [jnp.concatenate([vt_scr[h, j + b] for b in range(span)], axis=1) if span > 1 else vt_scr[h, j]
              for h in heads]
        acc = [acc[h] + _dot(vt[h], a[h].astype(BF16)) for h in heads]
        return tuple(v for h in heads for v in (tail[h], acc[h]))

    kj = lax.broadcasted_iota(jnp.int32, (tq, tq), 0)
    qi = lax.broadcasted_iota(jnp.int32, (tq, tq), 1)
    init = (jnp.zeros((1, tq), F32), jnp.zeros((HEAD_COLS, tq), F32)) * SB_HEADS
    carry = block(i, init, kj < qi)
    odd = i % span
    carry = lax.fori_loop(0, odd, lambda t, c: block(i - 1 - t, c, None), carry)
    out = lax.fori_loop(0, i // span, lambda t, c: block(i - odd - span * (t + 1), c, None, span), carry)
    for h in heads:
        o_ref[:, _head_cols(h)] = out[2 * h + 1].T.astype(BF16)


def _sb_attn(proj, *, batch, seq, tq=256, span=2):
    width = SB_HEADS * HEAD_COLS
    kv = lambda blk: pl.BlockSpec((None, seq, width), lambda b, i: (b, 0, blk))
    return pl.pallas_call(
        functools.partial(_sb_body, tq=tq, span=span),
        grid=(batch, seq // tq),
        in_specs=[pl.BlockSpec((None, tq, width), lambda b, i: (b, i, 3)), kv(4), kv(5)],
        out_specs=pl.BlockSpec((None, tq, width), lambda b, i: (b, i, 0)),
        out_shape=jax.ShapeDtypeStruct((batch, seq, width), BF16),
        scratch_shapes=[pltpu.VMEM((SB_HEADS, seq // tq, HEAD_COLS, tq), BF16)],
        compiler_params=_cparams("parallel", "arbitrary"),
        name="sb_attn",
    )(proj, proj, proj)


def _gdn_in_body(x_ref, xp_ref, nw_ref, wqkv_ref, wg_ref, wba_ref, cw_ref, alog_ref, dt_ref,
                 q_ref, k_ref, v_ref, gate_ref, bg_ref, p_scr, *, tm, seq, n_chunk):
    i = pl.program_id(0)
    nw = nw_ref[...]
    h = _rms(x_ref[...], nw).astype(BF16)
    seq_start = (i * tm) % seq == 0
    hp = jnp.where(seq_start, 0.0, _rms(xp_ref[...], nw)).astype(BF16)
    mix = GDN_HEADS * HEAD_COLS
    outs = (q_ref, k_ref, v_ref)
    q_scale = HEAD_COLS ** -0.5
    for c in range(0, 3 * mix, n_chunk):
        p_scr[0:CONV_HALO, :] = _dot(hp, wqkv_ref[:, c:c + n_chunk])
        p_scr[CONV_HALO:CONV_HALO + tm, :] = _dot(h, wqkv_ref[:, c:c + n_chunk])
        for s in range(0, n_chunk, HEAD_COLS):
            col = c + s
            win = p_scr[CONV_HALO - CONV_LEAD:CONV_HALO + tm, s:s + HEAD_COLS]
            y = None
            for tap in range(GDN_CONV):
                delay = GDN_CONV - 1 - tap
                shifted = pltpu.roll(win, delay, axis=0) if delay else win
                term = shifted[CONV_LEAD:, :] * cw_ref[tap:tap + 1, col:col + HEAD_COLS]
                y = term if y is None else y + term
            y = _silu(y)
            which, local = divmod(col, mix)
            if which < 2:
                y = y * lax.rsqrt(jnp.sum(y * y, axis=-1, keepdims=True) + L2_EPS)
            if which == 0:
                y = y * q_scale
            outs[which][:, local:local + HEAD_COLS] = y.astype(BF16)
    for c in range(0, mix, n_chunk):
        gate_ref[:, c:c + n_chunk] = _dot(h, wg_ref[:, c:c + n_chunk]).astype(BF16)
    ba = _dot(h, wba_ref[...])
    lane = lax.broadcasted_iota(jnp.int32, ba.shape, 1)
    beta = jax.nn.sigmoid(ba)
    g = -jnp.exp(alog_ref[...]) * _softplus(ba + dt_ref[...])
    bg_ref[...] = jnp.where(lane < GDN_HEADS, beta, jnp.where(lane < 2 * GDN_HEADS, g, 0.0))


def _gdn_in(x, nw, w_in, conv_w, a_log, dt_bias, *, seq, tm=512, n_chunk=512):
    m, d = x.shape
    mix = GDN_HEADS * HEAD_COLS
    w_qkv = w_in[:, :3 * mix].astype(BF16)
    w_gate = w_in[:, 3 * mix:4 * mix].astype(BF16)
    pad = HEAD_COLS - 2 * GDN_HEADS
    w_ba = jnp.pad(w_in[:, 4 * mix:], ((0, 0), (0, pad))).astype(BF16)
    lanes = lambda v: jnp.pad(v.astype(F32), (GDN_HEADS, pad)).reshape(1, HEAD_COLS)
    row = lambda n: pl.BlockSpec((tm, n), lambda i: (i, 0))
    halo_blocks = tm // CONV_HALO
    act = jax.ShapeDtypeStruct((m, mix), BF16)
    return pl.pallas_call(
        functools.partial(_gdn_in_body, tm=tm, seq=seq, n_chunk=n_chunk),
        grid=(m // tm,),
        in_specs=[row(d),
                  pl.BlockSpec((CONV_HALO, d), lambda i: (jnp.maximum(i * halo_blocks - 1, 0), 0)),
                  _resident((1, d)), _resident((d, 3 * mix)), _resident((d, mix)), _resident((d, HEAD_COLS)),
                  _resident((GDN_CONV, 3 * mix)), _resident((1, HEAD_COLS)), _resident((1, HEAD_COLS))],
        out_specs=[row(mix), row(mix), row(mix), row(mix), row(HEAD_COLS)],
        out_shape=[act, act, act, act, jax.ShapeDtypeStruct((m, HEAD_COLS), F32)],
        scratch_shapes=[pltpu.VMEM((CONV_HALO + tm, n_chunk), F32)],
        compiler_params=_cparams("parallel"),
        name="gdn_in",
    )(x, x, nw.reshape(1, d), w_qkv, w_gate, w_ba, conv_w.astype(F32), lanes(a_log), lanes(dt_bias))


def _split3(x):
    b1 = x.astype(BF16)
    r1 = x - b1.astype(F32)
    b2 = r1.astype(BF16)
    b3 = (r1 - b2.astype(F32)).astype(BF16)
    return b1, b2, b3


def _hi_lo(x):
    hi = x.astype(BF16)
    return hi, (x - hi.astype(F32)).astype(BF16)


def _dot_split(a_parts, b_parts):
    ah, al = a_parts
    bh, bl = b_parts
    c = bh.shape[0]
    rhs = jnp.concatenate([jnp.concatenate([bh, bl], axis=1),
                           jnp.concatenate([bh, jnp.zeros_like(bh)], axis=1)], axis=0)
    r = _dot(jnp.concatenate([ah, al], axis=1), rhs)
    return r[:, :c] + r[:, c:]


def _unit_lower_inverse(lows, eye):
    c = lows[0].shape[0]
    n = range(len(lows))
    low_parts = [_hi_lo(low) for low in lows]
    pw = [_dot_split(low_parts[i], low_parts[i]) for i in n]
    t_inv = [eye - low for low in lows]
    for _ in range(int(math.log2(c)) - 2):
        pw_parts = [_hi_lo(p) for p in pw]
        t_parts = [_hi_lo(t) for t in t_inv]
        stacked = [tuple(jnp.concatenate([t, p], axis=0) for t, p in zip(t_parts[i], pw_parts[i])) for i in n]
        both = [_dot_split(stacked[i], pw_parts[i]) for i in n]
        t_inv = [t_inv[i] + both[i][:c] for i in n]
        pw = [both[i][c:] for i in n]
    return [t_inv[i] + _dot_split(_hi_lo(t_inv[i]), _hi_lo(pw[i])) for i in n]


def _gdn_body(q_ref, k_ref, v_ref, gate_ref, bg_ref, nw_ref, o_ref, s_scr):
    c = GDN_CHUNK

    @pl.when(pl.program_id(1) == 0)
    def _():
        s_scr[...] = jnp.zeros_like(s_scr)

    row = lax.broadcasted_iota(jnp.int32, (c, c), 0)
    col = lax.broadcasted_iota(jnp.int32, (c, c), 1)
    incl = col <= row
    strict = col < row
    eye = (col == row).astype(F32)
    tri = incl.astype(BF16)
    bg = bg_ref[...]
    gc_all = sum(_dot(tri, part) for part in _split3(bg))
    gc_rows = gc_all.T
    nw = nw_ref[...]
    heads = range(GDN_HEADS)
    sl = [slice(h * HEAD_COLS, (h + 1) * HEAD_COLS) for h in heads]
    state = [s_scr[h] for h in heads]
    q = [q_ref[:, sl[h]].astype(F32) for h in heads]
    k = [k_ref[:, sl[h]].astype(F32) for h in heads]
    v = [v_ref[:, sl[h]].astype(F32) for h in heads]
    beta = [bg[:, h:h + 1] for h in heads]
    gcb = [jnp.broadcast_to(gc_all[:, GDN_HEADS + h:GDN_HEADS + h + 1], (c, c)) for h in heads]
    gct = [jnp.broadcast_to(gc_rows[GDN_HEADS + h:GDN_HEADS + h + 1, :], (c, c)) for h in heads]
    decay = [jnp.exp(jnp.where(incl, gcb[h] - gct[h], NEG_INF)) for h in heads]
    eg = [jnp.exp(gcb[h]) for h in heads]
    g_last = [gcb[h][c - 1:c, :] for h in heads]
    kb = [k[h] * beta[h] for h in heads]
    kbf = [k[h].astype(BF16) for h in heads]
    low = [jnp.where(strict, _dot_nt(kb[h].astype(BF16), kbf[h]) * decay[h], 0.0) for h in heads]
    aqk = [jnp.where(incl, _dot_nt(q[h].astype(BF16), kbf[h]) * decay[h], 0.0).astype(BF16) for h in heads]
    tb = [t.astype(BF16) for t in _unit_lower_inverse(low, eye)]
    u = [_dot(tb[h], (v[h] * beta[h]).astype(BF16)) for h in heads]
    w = [_dot(tb[h], (kb[h] * eg[h]).astype(BF16)).astype(BF16) for h in heads]
    sb = [state[h].astype(BF16) for h in heads]
    vnb = [(u[h] - _dot(w[h], sb[h])).astype(BF16) for h in heads]
    o = [_dot((q[h] * eg[h]).astype(BF16), sb[h]) + _dot(aqk[h], vnb[h]) for h in heads]
    k_dec = [(k[h] * jnp.exp(g_last[h] - gcb[h])).T.astype(BF16) for h in heads]
    new_state = [state[h] * jnp.exp(g_last[h]) + _dot(k_dec[h], vnb[h]) for h in heads]
    for h in heads:
        s_scr[h] = new_state[h]
        o_ref[:, sl[h]] = (_rms(o[h], nw) * _silu(gate_ref[:, sl[h]].astype(F32))).astype(BF16)


def _gdn(q, k, v, gate, bg, norm_w, *, batch, seq):
    mix = GDN_HEADS * HEAD_COLS
    c = GDN_CHUNK
    per_seq = seq // c
    row = lambda n: pl.BlockSpec((c, n), lambda b, t: (b * per_seq + t, 0))
    return pl.pallas_call(
        _gdn_body,
        grid=(batch, per_seq),
        in_specs=[row(mix), row(mix), row(mix), row(mix), row(HEAD_COLS),
                  pl.BlockSpec((1, HEAD_COLS), lambda b, t: (0, 0))],
        out_specs=row(mix),
        out_shape=jax.ShapeDtypeStruct((batch * seq, mix), BF16),
        scratch_shapes=[pltpu.VMEM((GDN_HEADS, HEAD_COLS, HEAD_COLS), F32)],
        compiler_params=_cparams("parallel", "arbitrary"),
        name="gdn",
    )(q, k, v, gate, bg, norm_w.reshape(1, HEAD_COLS))


def _even_mixer(x, norm_w, w_in, diff_lambda, diff_subln, w_out, layer_idx, *, batch, seq):
    n_in = w_in.shape[1]
    q_cols = DIFF_HEADS * HEAD_COLS
    sb_q0 = 3 * q_cols
    col = jnp.arange(n_in)
    log2e = math.log2(math.e)
    col_scale = jnp.where(col < q_cols, DIFF_QK_DIM ** -0.5 * log2e,
                          jnp.where((col >= sb_q0) & (col < sb_q0 + SB_HEADS * HEAD_COLS),
                                    HEAD_COLS ** -0.5 * log2e, 1.0))
    proj = _att_in(x, norm_w, w_in, col_scale.astype(F32).reshape(1, n_in)).reshape(batch, seq, n_in)
    lambda_init = 0.8 - 0.6 * math.exp(-0.3 * layer_idx)
    lp = diff_lambda.astype(F32)
    lam = jnp.exp(jnp.sum(lp[0] * lp[1])) - jnp.exp(jnp.sum(lp[2] * lp[3])) + lambda_init
    slopes2 = 2.0 ** (-8.0 * jnp.arange(1, DIFF_HEADS + 1, dtype=F32) / DIFF_HEADS) * log2e
    kpos, qpos = _alibi_lanes(slopes2, seq)
    oa = _diff_attn(proj, slopes2, lam.reshape(1), diff_subln.astype(F32), kpos, qpos, batch=batch, seq=seq,
                    out_scale=1.0 - lambda_init)
    osb = _sb_attn(proj, batch=batch, seq=seq)
    m = batch * seq
    return (oa.reshape(m, -1), osb.reshape(m, -1)), (w_out[:q_cols], w_out[q_cols:])


def _odd_mixer(x, norm_w, w_in, conv_w, a_log, dt_bias, gdn_norm, w_out, *, batch, seq):
    q, k, v, gate, bg = _gdn_in(x, norm_w, w_in, conv_w, a_log, dt_bias, seq=seq)
    o = _gdn(q, k, v, gate, bg, gdn_norm.astype(F32), batch=batch, seq=seq)
    return (o,), (w_out,)


def kernel(x, ffn1_norm, ffn1_w_gate, ffn1_w_up, ffn1_w_down, mix_norm, att_w_in, diff_lambda, diff_subln, att_w_out, gdn_w_in, gdn_conv_w, gdn_a_log, gdn_dt_bias, gdn_norm, gdn_w_out, ffn2_norm, ffn2_w_gate, ffn2_w_up, ffn2_w_down, final_norm):
    batch, seq, d = x.shape
    depth = ffn1_norm.shape[0]
    xf = x.reshape(batch * seq, d)
    for layer in range(depth):
        xf = _ffn(xf, ffn1_norm[layer], ffn1_w_gate[layer], ffn1_w_up[layer], ffn1_w_down[layer])
        if layer % 2 == 0:
            e = layer // 2
            mix, w_out = _even_mixer(xf, mix_norm[layer], att_w_in[e], diff_lambda[e], diff_subln[e], att_w_out[e],
                                     layer, batch=batch, seq=seq)
        else:
            o = layer // 2
            mix, w_out = _odd_mixer(xf, mix_norm[layer], gdn_w_in[o], gdn_conv_w[o], gdn_a_log[o], gdn_dt_bias[o],
                                    gdn_norm[o], gdn_w_out[o], batch=batch, seq=seq)
        last = layer == depth - 1
        xf = _ffn(xf, ffn2_norm[layer], ffn2_w_gate[layer], ffn2_w_up[layer], ffn2_w_down[layer],
                  final_w=final_norm if last else None, mix=mix, w_out=w_out)
    return xf.reshape(batch, seq, d)
```

```python
import functools
import math

import jax
import jax.numpy as jnp
from jax import lax
from jax.experimental import pallas as pl
from jax.experimental.pallas import tpu as pltpu

F32 = jnp.float32
BF16 = jnp.bfloat16

D_FF = 2816
RMS_EPS = 1e-6
L2_EPS = 1e-6
NEG_INF = -1e30

MASK_CHUNK_LOG2 = 6
DIFF_HEADS = 4
DIFF_QK_DIM = 64
SB_HEADS = 4
HEAD_COLS = 128
GDN_HEADS = 8
GDN_CONV = 4
GDN_CHUNK = 128

V7X_VMEM_LIMIT_BYTES = 56 * 1024 * 1024
CONV_HALO = 16
CONV_LEAD = 8


def _cparams(*semantics):
    return pltpu.CompilerParams(dimension_semantics=semantics, vmem_limit_bytes=V7X_VMEM_LIMIT_BYTES)


def _resident(shape):
    zeros = (0,) * len(shape)
    return pl.BlockSpec(shape, lambda *_: zeros, pipeline_mode=pl.Buffered(1))


def _rms(x, w):
    ms = jnp.mean(x * x, axis=-1, keepdims=True)
    return x * lax.rsqrt(ms + RMS_EPS) * w


def _silu(x):
    h = 0.5 * x
    return h + h * jnp.tanh(h)


def _softplus(x):
    return jnp.maximum(x, 0.0) + jnp.log(1.0 + jnp.exp(-jnp.abs(x)))


def _dot(a, b):
    return jnp.dot(a, b, preferred_element_type=F32)


def _dot_nt(a, b):
    return lax.dot_general(a, b, (((1,), (1,)), ((), ())), preferred_element_type=F32)


def _ffn_body(*refs, ff_chunk, n_mix, final):
    x_ref, nw_ref, wg_ref, wu_ref, wd_ref = refs[:5]
    mix_refs = refs[5:5 + n_mix]
    wout_refs = refs[5 + n_mix:5 + 2 * n_mix]
    rest = refs[5 + 2 * n_mix:]
    if final:
        fw_ref, o_ref, a_ref = rest
    else:
        o_ref, a_ref = rest
    x = x_ref[...]
    for mix_ref, wout_ref in zip(mix_refs, wout_refs):
        x = x + _dot(mix_ref[...], wout_ref[...])
    h = _rms(x, nw_ref[...]).astype(BF16)
    for c in range(0, D_FF, ff_chunk):
        g = _dot(h, wg_ref[:, c:c + ff_chunk].astype(BF16))
        u = _dot(h, wu_ref[:, c:c + ff_chunk].astype(BF16))
        a_ref[:, c:c + ff_chunk] = (_silu(g) * u).astype(BF16)
    y = x + 0.5 * _dot(a_ref[...], wd_ref[...].astype(BF16))
    if final:
        y = _rms(y, fw_ref[...])
    o_ref[...] = y


def _ffn(x, nw, wg, wu, wd, final_w=None, mix=(), w_out=(), *, tm=512, ff_chunk=256):
    m, d = x.shape
    final = final_w is not None
    row = pl.BlockSpec((tm, d), lambda i: (i, 0))
    in_specs = [row, _resident((1, d)), _resident((d, D_FF)), _resident((d, D_FF)), _resident((D_FF, d))]
    in_specs += [pl.BlockSpec((tm, a.shape[1]), lambda i: (i, 0)) for a in mix]
    in_specs += [_resident(w.shape) for w in w_out]
    args = [x, nw.reshape(1, d), wg, wu, wd, *mix, *[w.astype(BF16) for w in w_out]]
    if final:
        in_specs.append(_resident((1, d)))
        args.append(final_w.reshape(1, d))
    return pl.pallas_call(
        functools.partial(_ffn_body, ff_chunk=ff_chunk, n_mix=len(mix), final=final),
        grid=(m // tm,),
        in_specs=in_specs,
        out_specs=row,
        out_shape=jax.ShapeDtypeStruct((m, d), F32),
        scratch_shapes=[pltpu.VMEM((tm, D_FF), BF16)],
        compiler_params=_cparams("parallel"),
        name="ffn",
    )(*args)


def _att_in_body(x_ref, nw_ref, w_ref, sc_ref, o_ref, *, n_chunk):
    h = _rms(x_ref[...], nw_ref[...]).astype(BF16)
    n = w_ref.shape[1]
    for c in range(0, n, n_chunk):
        p = _dot(h, w_ref[:, c:c + n_chunk].astype(BF16))
        o_ref[:, c:c + n_chunk] = (p * sc_ref[:, c:c + n_chunk]).astype(BF16)


def _att_in(x, nw, w, col_scale, *, tm=512, n_chunk=512):
    m, d = x.shape
    n = w.shape[1]
    return pl.pallas_call(
        functools.partial(_att_in_body, n_chunk=n_chunk),
        grid=(m // tm,),
        in_specs=[pl.BlockSpec((tm, d), lambda i: (i, 0)), _resident((1, d)), _resident((d, n)),
                  _resident((1, n))],
        out_specs=pl.BlockSpec((tm, n), lambda i: (i, 0)),
        out_shape=jax.ShapeDtypeStruct((m, n), BF16),
        compiler_params=_cparams("parallel"),
        name="att_in",
    )(x, nw.reshape(1, d), w, col_scale)


def _head_cols(h):
    return slice(h * HEAD_COLS, (h + 1) * HEAD_COLS)


def _store_v_transposed(v_ref, vt_scr, tk):
    for h in range(vt_scr.shape[0]):
        for j in range(vt_scr.shape[1]):
            vt_scr[h, j] = v_ref[j * tk:(j + 1) * tk, _head_cols(h)].astype(F32).T.astype(BF16)


def _diff_body(slope_ref, lam_ref, q_ref, k_ref, v_ref, kpos_ref, qpos_ref, sw_ref, o_ref, vt_scr, acc_scr, *, tq,
               span, out_scale):
    i = pl.program_id(1)
    lam = lam_ref[0]
    heads = range(DIFF_HEADS)
    chains = [(h, c) for h in heads for c in range(2)]

    @pl.when(i == 0)
    def _():
        _store_v_transposed(v_ref, vt_scr, tq)

    lane = lax.broadcasted_iota(jnp.int32, (tq, HEAD_COLS), 1)
    first = lane < DIFF_QK_DIM
    q_cat = []
    for h in heads:
        q = q_ref[:, _head_cols(h)]
        zero = jnp.zeros_like(q)
        qpos = jnp.broadcast_to(qpos_ref[h], q.shape)
        q_cat += [jnp.concatenate([jnp.where(first, q, zero), qpos], axis=1),
                  jnp.concatenate([jnp.where(first, zero, q), qpos], axis=1)]

    q_cat_t = [qc.astype(F32).T.astype(BF16) for qc in q_cat]

    def block(j, carry, fixups, span=1):
        n = range(len(chains))
        start = pl.multiple_of(j * tq, tq)
        kpos = kpos_ref[pl.ds(start, span * tq), :]
        k_cat = [jnp.concatenate([k_ref[pl.ds(start, span * tq), _head_cols(h)], kpos], axis=1) for h in heads]
        vt = [jnp.concatenate([vt_scr[h, j + b] for b in range(span)], axis=1) if span > 1 else vt_scr[h, j]
              for h in heads]
        ones = jnp.ones((SUM_ROWS, span * tq), BF16)
        vt = [jnp.concatenate([vt[h], ones], axis=0) for h in heads]
        m = carry
        s = [_dot(k_cat[h], q_cat_t[x]) for x, (h, _) in enumerate(chains)]
        if fixups is not None:
            s = [s[x] + fixups[h] for x, (h, _) in enumerate(chains)]
        m_new = [jnp.maximum(m[x], jnp.max(s[x], axis=0, keepdims=True)) for x in n]
        alpha = [jnp.exp2(m[x] - m_new[x]) for x in n]
        p = [jnp.exp2(s[x] - m_new[x]) for x in n]
        pv = [_dot(vt[h], p[x].astype(BF16)) for x, (h, _) in enumerate(chains)]
        for x in n:
            acc_scr[x] = alpha[x] * acc_scr[x] + pv[x]
        return tuple(m_new)

    acc_scr[...] = jnp.zeros_like(acc_scr)
    init = (jnp.full((1, tq), NEG_INF, F32),) * len(chains)
    carry = lax.fori_loop(0, i // span, lambda j, c: block(span * j, c, None, span), init)
    carry = lax.fori_loop(span * (i // span), i, lambda j, c: block(j, c, None), carry)
    kj = lax.broadcasted_iota(jnp.int32, (tq, tq), 0)
    qi = lax.broadcasted_iota(jnp.int32, (tq, tq), 1)
    ahead = jnp.where(kj > qi, (qi - kj).astype(F32), 0.0)
    allowed = (kj >> MASK_CHUNK_LOG2) <= (qi >> MASK_CHUNK_LOG2)
    fixups = [jnp.where(allowed, (2.0 * slope_ref[h]) * ahead, NEG_INF) for h in heads]
    block(i, carry, fixups)
    sw = sw_ref[...]
    for h in heads:
        a0, a1 = acc_scr[2 * h], acc_scr[2 * h + 1]
        l0, l1 = a0[HEAD_COLS:HEAD_COLS + 1, :], a1[HEAD_COLS:HEAD_COLS + 1, :]
        o = (a0[:HEAD_COLS] * (1.0 / l0) - (lam / l1) * a1[:HEAD_COLS]).T
        o_ref[:, _head_cols(h)] = (_rms(o, sw) * out_scale).astype(BF16)


def _diff_attn(proj, slopes2, lam, subln, kpos, qpos, *, batch, seq, out_scale, tq=256, span=2):
    smem = pl.BlockSpec(memory_space=pltpu.SMEM)
    width = DIFF_HEADS * HEAD_COLS
    kv = lambda blk: pl.BlockSpec((None, seq, width), lambda b, i: (b, 0, blk))
    return pl.pallas_call(
        functools.partial(_diff_body, tq=tq, span=span, out_scale=out_scale),
        grid=(batch, seq // tq),
        in_specs=[smem, smem,
                  pl.BlockSpec((None, tq, width), lambda b, i: (b, i, 0)),
                  kv(1), kv(2),
                  pl.BlockSpec((seq, HEAD_COLS), lambda b, i: (0, 0)),
                  pl.BlockSpec((DIFF_HEADS, 1, HEAD_COLS), lambda b, i: (0, 0, 0)),
                  pl.BlockSpec((1, HEAD_COLS), lambda b, i: (0, 0))],
        out_specs=pl.BlockSpec((None, tq, width), lambda b, i: (b, i, 0)),
        out_shape=jax.ShapeDtypeStruct((batch, seq, width), BF16),
        scratch_shapes=[pltpu.VMEM((DIFF_HEADS, seq // tq, HEAD_COLS, tq), BF16),
                        pltpu.VMEM((2 * DIFF_HEADS, HEAD_COLS + SUM_ROWS, tq), F32)],
        compiler_params=_cparams("parallel", "arbitrary"),
        name="diff_attn",
    )(slopes2, lam, proj, proj, proj, kpos, qpos, subln.reshape(1, HEAD_COLS))


def _alibi_lanes(slopes2, seq):
    pos = jnp.arange(seq)
    a = (pos >> MASK_CHUNK_LOG2).astype(F32)
    b = (pos & ((1 << MASK_CHUNK_LOG2) - 1)).astype(F32)
    kpos = jnp.zeros((seq, HEAD_COLS), F32).at[:, 0:3].set(a[:, None]).at[:, 3:6].set(b[:, None]).astype(BF16)

    def split3(x):
        x1 = x.astype(BF16).astype(F32)
        x2 = (x - x1).astype(BF16).astype(F32)
        return jnp.stack([x1, x2, (x - x1 - x2).astype(BF16).astype(F32)], axis=-1)

    mult = jnp.concatenate([split3(slopes2 * (1 << MASK_CHUNK_LOG2)), split3(slopes2)], axis=-1)
    qpos = jnp.zeros((slopes2.shape[0], 1, HEAD_COLS), F32).at[:, 0, 0:6].set(mult).astype(BF16)
    return kpos, qpos


SUM_ROWS = 16
SB_GROUP = 128


def _sb_body(q_ref, k_ref, v_ref, o_ref, vt_scr, acc_scr, *, tq, span):
    i = pl.program_id(1)
    heads = range(SB_HEADS)

    @pl.when(i == 0)
    def _():
        _store_v_transposed(v_ref, vt_scr, tq)

    q = [q_ref[:, _head_cols(h)] for h in heads]
    gs = lax.broadcasted_iota(jnp.int32, (SB_GROUP, SB_GROUP), 0)
    gj = lax.broadcasted_iota(jnp.int32, (SB_GROUP, SB_GROUP), 1)
    later = (gj > gs).astype(BF16)
    later2 = jnp.concatenate([later, later], axis=1)
    later2 = jnp.concatenate([later2, jnp.ones((SUM_ROWS, 2 * SB_GROUP), BF16)], axis=0)

    def block(j, carry, earlier, span=1):
        tail = list(carry)
        start = pl.multiple_of(j * tq, tq)
        groups = [slice(g * SB_GROUP, (g + 1) * SB_GROUP) for g in range(span * tq // SB_GROUP)]
        z = [_dot_nt(k_ref[pl.ds(start, span * tq), _head_cols(h)], q[h]) for h in heads]
        neg_keep = [jnp.maximum(z[h], 0.0) + jnp.log2(1.0 + jnp.exp2(-jnp.abs(z[h]))) for h in heads]
        log_beta = [z[h] - neg_keep[h] for h in heads]
        if earlier is not None:
            neg_keep = [jnp.where(earlier, neg_keep[h], 0.0) for h in heads]
        after = [[None] * len(groups) for _ in heads]
        for g in reversed(range(len(groups))):
            parts = [_hi_lo(neg_keep[h][groups[g], :]) for h in heads]
            within = [_dot(later2, jnp.concatenate(parts[h], axis=0)) for h in heads]
            for h in heads:
                after[h][g] = within[h][:SB_GROUP] + tail[h]
                tail[h] = tail[h] + within[h][SB_GROUP:SB_GROUP + 1]
        a = [jnp.exp2(log_beta[h] - jnp.concatenate(after[h], axis=0)) for h in heads]
        if earlier is not None:
            a = [jnp.where(earlier, a[h], 0.0) for h in heads]
        vt = [jnp.concatenate([vt_scr[h, j + b] for b in range(span)], axis=1) if span > 1 else vt_scr[h, j]
              for h in heads]
        for h in heads:
            acc_scr[h] += _dot(vt[h], a[h].astype(BF16))
        return tuple(tail)

    kj = lax.broadcasted_iota(jnp.int32, (tq, tq), 0)
    qi = lax.broadcasted_iota(jnp.int32, (tq, tq), 1)
    acc_scr[...] = jnp.zeros_like(acc_scr)
    init = (jnp.zeros((1, tq), F32),) * SB_HEADS
    carry = block(i, init, kj < qi)
    odd = i % span
    carry = lax.fori_loop(0, odd, lambda t, c: block(i - 1 - t, c, None), carry)
    lax.fori_loop(0, i // span, lambda t, c: block(i - odd - span * (t + 1), c, None, span), carry)
    for h in heads:
        o_ref[:, _head_cols(h)] = acc_scr[h].T.astype(BF16)


def _sb_attn(proj, *, batch, seq, tq=256, span=2):
    width = SB_HEADS * HEAD_COLS
    kv = lambda blk: pl.BlockSpec((None, seq, width), lambda b, i: (b, 0, blk))
    return pl.pallas_call(
        functools.partial(_sb_body, tq=tq, span=span),
        grid=(batch, seq // tq),
        in_specs=[pl.BlockSpec((None, tq, width), lambda b, i: (b, i, 3)), kv(4), kv(5)],
        out_specs=pl.BlockSpec((None, tq, width), lambda b, i: (b, i, 0)),
        out_shape=jax.ShapeDtypeStruct((batch, seq, width), BF16),
        scratch_shapes=[pltpu.VMEM((SB_HEADS, seq // tq, HEAD_COLS, tq), BF16),
                        pltpu.VMEM((SB_HEADS, HEAD_COLS, tq), F32)],
        compiler_params=_cparams("parallel", "arbitrary"),
        name="sb_attn",
    )(proj, proj, proj)


def _gdn_in_body(x_ref, xp_ref, nw_ref, wqkv_ref, wg_ref, wba_ref, cw_ref, alog_ref, dt_ref,
                 q_ref, k_ref, v_ref, gate_ref, bg_ref, p_scr, *, tm, seq, n_chunk):
    i = pl.program_id(0)
    nw = nw_ref[...]
    h = _rms(x_ref[...], nw).astype(BF16)
    seq_start = (i * tm) % seq == 0
    hp = jnp.where(seq_start, 0.0, _rms(xp_ref[...], nw)).astype(BF16)
    mix = GDN_HEADS * HEAD_COLS
    outs = (q_ref, k_ref, v_ref)
    q_scale = HEAD_COLS ** -0.5
    for c in range(0, 3 * mix, n_chunk):
        p_scr[0:CONV_HALO, :] = _dot(hp, wqkv_ref[:, c:c + n_chunk])
        p_scr[CONV_HALO:CONV_HALO + tm, :] = _dot(h, wqkv_ref[:, c:c + n_chunk])
        for s in range(0, n_chunk, HEAD_COLS):
            col = c + s
            win = p_scr[CONV_HALO - CONV_LEAD:CONV_HALO + tm, s:s + HEAD_COLS]
            y = None
            for tap in range(GDN_CONV):
                delay = GDN_CONV - 1 - tap
                shifted = pltpu.roll(win, delay, axis=0) if delay else win
                term = shifted[CONV_LEAD:, :] * cw_ref[tap:tap + 1, col:col + HEAD_COLS]
                y = term if y is None else y + term
            y = _silu(y)
            which, local = divmod(col, mix)
            if which < 2:
                y = y * lax.rsqrt(jnp.sum(y * y, axis=-1, keepdims=True) + L2_EPS)
            if which == 0:
                y = y * q_scale
            outs[which][:, local:local + HEAD_COLS] = y.astype(BF16)
    for c in range(0, mix, n_chunk):
        gate_ref[:, c:c + n_chunk] = _dot(h, wg_ref[:, c:c + n_chunk]).astype(BF16)
    ba = _dot(h, wba_ref[...])
    lane = lax.broadcasted_iota(jnp.int32, ba.shape, 1)
    beta = jax.nn.sigmoid(ba)
    g = -jnp.exp(alog_ref[...]) * _softplus(ba + dt_ref[...])
    bg_ref[...] = jnp.where(lane < GDN_HEADS, beta, jnp.where(lane < 2 * GDN_HEADS, g, 0.0))


def _gdn_in(x, nw, w_in, conv_w, a_log, dt_bias, *, seq, tm=512, n_chunk=512):
    m, d = x.shape
    mix = GDN_HEADS * HEAD_COLS
    w_qkv = w_in[:, :3 * mix].astype(BF16)
    w_gate = w_in[:, 3 * mix:4 * mix].astype(BF16)
    pad = HEAD_COLS - 2 * GDN_HEADS
    w_ba = jnp.pad(w_in[:, 4 * mix:], ((0, 0), (0, pad))).astype(BF16)
    lanes = lambda v: jnp.pad(v.astype(F32), (GDN_HEADS, pad)).reshape(1, HEAD_COLS)
    row = lambda n: pl.BlockSpec((tm, n), lambda i: (i, 0))
    halo_blocks = tm // CONV_HALO
    act = jax.ShapeDtypeStruct((m, mix), BF16)
    return pl.pallas_call(
        functools.partial(_gdn_in_body, tm=tm, seq=seq, n_chunk=n_chunk),
        grid=(m // tm,),
        in_specs=[row(d),
                  pl.BlockSpec((CONV_HALO, d), lambda i: (jnp.maximum(i * halo_blocks - 1, 0), 0)),
                  _resident((1, d)), _resident((d, 3 * mix)), _resident((d, mix)), _resident((d, HEAD_COLS)),
                  _resident((GDN_CONV, 3 * mix)), _resident((1, HEAD_COLS)), _resident((1, HEAD_COLS))],
        out_specs=[row(mix), row(mix), row(mix), row(mix), row(HEAD_COLS)],
        out_shape=[act, act, act, act, jax.ShapeDtypeStruct((m, HEAD_COLS), F32)],
        scratch_shapes=[pltpu.VMEM((CONV_HALO + tm, n_chunk), F32)],
        compiler_params=_cparams("parallel"),
        name="gdn_in",
    )(x, x, nw.reshape(1, d), w_qkv, w_gate, w_ba, conv_w.astype(F32), lanes(a_log), lanes(dt_bias))


def _split3(x):
    b1 = x.astype(BF16)
    r1 = x - b1.astype(F32)
    b2 = r1.astype(BF16)
    b3 = (r1 - b2.astype(F32)).astype(BF16)
    return b1, b2, b3


def _hi_lo(x):
    hi = x.astype(BF16)
    return hi, (x - hi.astype(F32)).astype(BF16)


def _dot_split(a_parts, b_parts):
    ah, al = a_parts
    bh, bl = b_parts
    c = bh.shape[0]
    rhs = jnp.concatenate([jnp.concatenate([bh, bl], axis=1),
                           jnp.concatenate([bh, jnp.zeros_like(bh)], axis=1)], axis=0)
    r = _dot(jnp.concatenate([ah, al], axis=1), rhs)
    return r[:, :c] + r[:, c:]


def _unit_lower_inverse(lows, eye):
    c = lows[0].shape[0]
    n = range(len(lows))
    row = lax.broadcasted_iota(jnp.int32, (c, c), 0)
    col = lax.broadcasted_iota(jnp.int32, (c, c), 1)
    t_inv = None
    for level in range(int(math.log2(c))):
        lower_left = ((row >> (level + 1)) == (col >> (level + 1))) & (((row >> level) & 1) == 1) \
            & (((col >> level) & 1) == 0)
        off = [jnp.where(lower_left, low, 0.0) for low in lows]
        if t_inv is None:
            t_inv = [eye - off[i] for i in n]
            continue
        x = [_dot_split(_hi_lo(off[i]), _hi_lo(t_inv[i])) for i in n]
        t_inv = [t_inv[i] - _dot_split(_hi_lo(t_inv[i]), _hi_lo(x[i])) for i in n]
    return t_inv


def _gdn_body(q_ref, k_ref, v_ref, gate_ref, bg_ref, nw_ref, o_ref, s_scr):
    c = GDN_CHUNK

    @pl.when(pl.program_id(1) == 0)
    def _():
        s_scr[...] = jnp.zeros_like(s_scr)

    row = lax.broadcasted_iota(jnp.int32, (c, c), 0)
    col = lax.broadcasted_iota(jnp.int32, (c, c), 1)
    incl = col <= row
    strict = col < row
    eye = (col == row).astype(F32)
    tri = incl.astype(BF16)
    bg = bg_ref[...]
    gc_all = sum(_dot(tri, part) for part in _split3(bg))
    gc_rows = gc_all.T
    nw = nw_ref[...]
    heads = range(GDN_HEADS)
    sl = [slice(h * HEAD_COLS, (h + 1) * HEAD_COLS) for h in heads]
    state = [s_scr[h] for h in heads]
    q = [q_ref[:, sl[h]].astype(F32) for h in heads]
    k = [k_ref[:, sl[h]].astype(F32) for h in heads]
    v = [v_ref[:, sl[h]].astype(F32) for h in heads]
    beta = [bg[:, h:h + 1] for h in heads]
    gcb = [jnp.broadcast_to(gc_all[:, GDN_HEADS + h:GDN_HEADS + h + 1], (c, c)) for h in heads]
    gct = [jnp.broadcast_to(gc_rows[GDN_HEADS + h:GDN_HEADS + h + 1, :], (c, c)) for h in heads]
    decay = [jnp.exp(jnp.where(incl, gcb[h] - gct[h], NEG_INF)) for h in heads]
    eg = [jnp.exp(gcb[h]) for h in heads]
    g_last = [gcb[h][c - 1:c, :] for h in heads]
    kb = [k[h] * beta[h] for h in heads]
    kbf = [k[h].astype(BF16) for h in heads]
    low = [jnp.where(strict, _dot_nt(kb[h].astype(BF16), kbf[h]) * decay[h], 0.0) for h in heads]
    aqk = [jnp.where(incl, _dot_nt(q[h].astype(BF16), kbf[h]) * decay[h], 0.0).astype(BF16) for h in heads]
    tb = [t.astype(BF16) for t in _unit_lower_inverse(low, eye)]
    u = [_dot(tb[h], (v[h] * beta[h]).astype(BF16)) for h in heads]
    w = [_dot(tb[h], (kb[h] * eg[h]).astype(BF16)).astype(BF16) for h in heads]
    sb = [state[h].astype(BF16) for h in heads]
    vnb = [(u[h] - _dot(w[h], sb[h])).astype(BF16) for h in heads]
    o = [_dot((q[h] * eg[h]).astype(BF16), sb[h]) + _dot(aqk[h], vnb[h]) for h in heads]
    k_dec = [(k[h] * jnp.exp(g_last[h] - gcb[h])).T.astype(BF16) for h in heads]
    new_state = [state[h] * jnp.exp(g_last[h]) + _dot(k_dec[h], vnb[h]) for h in heads]
    for h in heads:
        s_scr[h] = new_state[h]
        o_ref[:, sl[h]] = (_rms(o[h], nw) * _silu(gate_ref[:, sl[h]].astype(F32))).astype(BF16)


def _gdn(q, k, v, gate, bg, norm_w, *, batch, seq):
    mix = GDN_HEADS * HEAD_COLS
    c = GDN_CHUNK
    per_seq = seq // c
    row = lambda n: pl.BlockSpec((c, n), lambda b, t: (b * per_seq + t, 0))
    return pl.pallas_call(
        _gdn_body,
        grid=(batch, per_seq),
        in_specs=[row(mix), row(mix), row(mix), row(mix), row(HEAD_COLS),
                  pl.BlockSpec((1, HEAD_COLS), lambda b, t: (0, 0))],
        out_specs=row(mix),
        out_shape=jax.ShapeDtypeStruct((batch * seq, mix), BF16),
        scratch_shapes=[pltpu.VMEM((GDN_HEADS, HEAD_COLS, HEAD_COLS), F32)],
        compiler_params=_cparams("parallel", "arbitrary"),
        name="gdn",
    )(q, k, v, gate, bg, norm_w.reshape(1, HEAD_COLS))


def _even_mixer(x, norm_w, w_in, diff_lambda, diff_subln, w_out, layer_idx, *, batch, seq):
    n_in = w_in.shape[1]
    q_cols = DIFF_HEADS * HEAD_COLS
    sb_q0 = 3 * q_cols
    col = jnp.arange(n_in)
    log2e = math.log2(math.e)
    col_scale = jnp.where(col < q_cols, DIFF_QK_DIM ** -0.5 * log2e,
                          jnp.where((col >= sb_q0) & (col < sb_q0 + SB_HEADS * HEAD_COLS),
                                    HEAD_COLS ** -0.5 * log2e, 1.0))
    proj = _att_in(x, norm_w, w_in, col_scale.astype(F32).reshape(1, n_in)).reshape(batch, seq, n_in)
    lambda_init = 0.8 - 0.6 * math.exp(-0.3 * layer_idx)
    lp = diff_lambda.astype(F32)
    lam = jnp.exp(jnp.sum(lp[0] * lp[1])) - jnp.exp(jnp.sum(lp[2] * lp[3])) + lambda_init
    slopes2 = 2.0 ** (-8.0 * jnp.arange(1, DIFF_HEADS + 1, dtype=F32) / DIFF_HEADS) * log2e
    kpos, qpos = _alibi_lanes(slopes2, seq)
    oa = _diff_attn(proj, slopes2, lam.reshape(1), diff_subln.astype(F32), kpos, qpos, batch=batch, seq=seq,
                    out_scale=1.0 - lambda_init)
    osb = _sb_attn(proj, batch=batch, seq=seq)
    m = batch * seq
    return (oa.reshape(m, -1), osb.reshape(m, -1)), (w_out[:q_cols], w_out[q_cols:])


def _odd_mixer(x, norm_w, w_in, conv_w, a_log, dt_bias, gdn_norm, w_out, *, batch, seq):
    q, k, v, gate, bg = _gdn_in(x, norm_w, w_in, conv_w, a_log, dt_bias, seq=seq)
    o = _gdn(q, k, v, gate, bg, gdn_norm.astype(F32), batch=batch, seq=seq)
    return (o,), (w_out,)


def kernel(x, ffn1_norm, ffn1_w_gate, ffn1_w_up, ffn1_w_down, mix_norm, att_w_in, diff_lambda, diff_subln, att_w_out, gdn_w_in, gdn_conv_w, gdn_a_log, gdn_dt_bias, gdn_norm, gdn_w_out, ffn2_norm, ffn2_w_gate, ffn2_w_up, ffn2_w_down, final_norm):
    batch, seq, d = x.shape
    depth = ffn1_norm.shape[0]
    xf = x.reshape(batch * seq, d)
    for layer in range(depth):
        xf = _ffn(xf, ffn1_norm[layer], ffn1_w_gate[layer], ffn1_w_up[layer], ffn1_w_down[layer])
        if layer % 2 == 0:
            e = layer // 2
            mix, w_out = _even_mixer(xf, mix_norm[layer], att_w_in[e], diff_lambda[e], diff_subln[e], att_w_out[e],
                                     layer, batch=batch, seq=seq)
        else:
            o = layer // 2
            mix, w_out = _odd_mixer(xf, mix_norm[layer], gdn_w_in[o], gdn_conv_w[o], gdn_a_log[o], gdn_dt_bias[o],
                                    gdn_norm[o], gdn_w_out[o], batch=batch, seq=seq)
        last = layer == depth - 1
        xf = _ffn(xf, ffn2_norm[layer], ffn2_w_gate[layer], ffn2_w_up[layer], ffn2_w_down[layer],
                  final_w=final_norm if last else None, mix=mix, w_out=w_out)
    return xf.reshape(batch, seq, d)
```

```python
import functools
import math

import jax
import jax.numpy as jnp
from jax import lax
from jax.experimental import pallas as pl
from jax.experimental.pallas import tpu as pltpu

F32 = jnp.float32
BF16 = jnp.bfloat16

D_FF = 2816
RMS_EPS = 1e-6
L2_EPS = 1e-6
NEG_INF = -1e30

MASK_CHUNK_LOG2 = 6
DIFF_HEADS = 4
DIFF_QK_DIM = 64
SB_HEADS = 4
HEAD_COLS = 128
GDN_HEADS = 8
GDN_CONV = 4
GDN_CHUNK = 128

V7X_VMEM_LIMIT_BYTES = 56 * 1024 * 1024
CONV_HALO = 16
CONV_LEAD = 8


def _cparams(*semantics):
    return pltpu.CompilerParams(dimension_semantics=semantics, vmem_limit_bytes=V7X_VMEM_LIMIT_BYTES)


def _resident(shape):
    zeros = (0,) * len(shape)
    return pl.BlockSpec(shape, lambda *_: zeros, pipeline_mode=pl.Buffered(1))


def _rms(x, w):
    ms = jnp.mean(x * x, axis=-1, keepdims=True)
    return x * lax.rsqrt(ms + RMS_EPS) * w


def _silu(x):
    h = 0.5 * x
    return h + h * jnp.tanh(h)


def _softplus(x):
    return jnp.maximum(x, 0.0) + jnp.log(1.0 + jnp.exp(-jnp.abs(x)))


def _dot(a, b):
    return jnp.dot(a, b, preferred_element_type=F32)


def _dot_nt(a, b):
    return lax.dot_general(a, b, (((1,), (1,)), ((), ())), preferred_element_type=F32)


def _ffn_body(*refs, ff_chunk, n_mix, final):
    x_ref, nw_ref, wg_ref, wu_ref, wd_ref = refs[:5]
    mix_refs = refs[5:5 + n_mix]
    wout_refs = refs[5 + n_mix:5 + 2 * n_mix]
    rest = refs[5 + 2 * n_mix:]
    if final:
        fw_ref, o_ref, a_ref = rest
    else:
        o_ref, a_ref = rest
    x = x_ref[...]
    for mix_ref, wout_ref in zip(mix_refs, wout_refs):
        x = x + _dot(mix_ref[...], wout_ref[...])
    h = _rms(x, nw_ref[...]).astype(BF16)
    for c in range(0, D_FF, ff_chunk):
        g = _dot(h, wg_ref[:, c:c + ff_chunk].astype(BF16))
        u = _dot(h, wu_ref[:, c:c + ff_chunk].astype(BF16))
        a_ref[:, c:c + ff_chunk] = (_silu(g) * u).astype(BF16)
    y = x + 0.5 * _dot(a_ref[...], wd_ref[...].astype(BF16))
    if final:
        y = _rms(y, fw_ref[...])
    o_ref[...] = y


def _ffn(x, nw, wg, wu, wd, final_w=None, mix=(), w_out=(), *, tm=512, ff_chunk=256):
    m, d = x.shape
    final = final_w is not None
    row = pl.BlockSpec((tm, d), lambda i: (i, 0))
    in_specs = [row, _resident((1, d)), _resident((d, D_FF)), _resident((d, D_FF)), _resident((D_FF, d))]
    in_specs += [pl.BlockSpec((tm, a.shape[1]), lambda i: (i, 0)) for a in mix]
    in_specs += [_resident(w.shape) for w in w_out]
    args = [x, nw.reshape(1, d), wg, wu, wd, *mix, *[w.astype(BF16) for w in w_out]]
    if final:
        in_specs.append(_resident((1, d)))
        args.append(final_w.reshape(1, d))
    return pl.pallas_call(
        functools.partial(_ffn_body, ff_chunk=ff_chunk, n_mix=len(mix), final=final),
        grid=(m // tm,),
        in_specs=in_specs,
        out_specs=row,
        out_shape=jax.ShapeDtypeStruct((m, d), F32),
        scratch_shapes=[pltpu.VMEM((tm, D_FF), BF16)],
        compiler_params=_cparams("parallel"),
        name="ffn",
    )(*args)


def _att_in_body(x_ref, nw_ref, w_ref, sc_ref, o_ref, *, n_chunk):
    h = _rms(x_ref[...], nw_ref[...]).astype(BF16)
    n = w_ref.shape[1]
    for c in range(0, n, n_chunk):
        p = _dot(h, w_ref[:, c:c + n_chunk].astype(BF16))
        o_ref[:, c:c + n_chunk] = (p * sc_ref[:, c:c + n_chunk]).astype(BF16)


def _att_in(x, nw, w, col_scale, *, tm=512, n_chunk=512):
    m, d = x.shape
    n = w.shape[1]
    return pl.pallas_call(
        functools.partial(_att_in_body, n_chunk=n_chunk),
        grid=(m // tm,),
        in_specs=[pl.BlockSpec((tm, d), lambda i: (i, 0)), _resident((1, d)), _resident((d, n)),
                  _resident((1, n))],
        out_specs=pl.BlockSpec((tm, n), lambda i: (i, 0)),
        out_shape=jax.ShapeDtypeStruct((m, n), BF16),
        compiler_params=_cparams("parallel"),
        name="att_in",
    )(x, nw.reshape(1, d), w, col_scale)


def _head_cols(h):
    return slice(h * HEAD_COLS, (h + 1) * HEAD_COLS)


def _store_v_transposed(v_ref, vt_scr, tk):
    for h in range(vt_scr.shape[0]):
        for j in range(vt_scr.shape[1]):
            vt_scr[h, j] = v_ref[j * tk:(j + 1) * tk, _head_cols(h)].astype(F32).T.astype(BF16)


def _diff_body(slope_ref, lam_ref, q_ref, k_ref, v_ref, kpos_ref, qpos_ref, sw_ref, o_ref, vt_scr, acc_scr, *, tq,
               span, out_scale):
    i = pl.program_id(1)
    lam = lam_ref[0]
    heads = range(DIFF_HEADS)
    chains = [(h, c) for h in heads for c in range(2)]

    @pl.when(i == 0)
    def _():
        _store_v_transposed(v_ref, vt_scr, tq)

    lane = lax.broadcasted_iota(jnp.int32, (tq, HEAD_COLS), 1)
    first = lane < DIFF_QK_DIM
    q_cat = []
    for h in heads:
        q = q_ref[:, _head_cols(h)]
        zero = jnp.zeros_like(q)
        qpos = jnp.broadcast_to(qpos_ref[h], q.shape)
        q_cat += [jnp.concatenate([jnp.where(first, q, zero), qpos], axis=1),
                  jnp.concatenate([jnp.where(first, zero, q), qpos], axis=1)]

    q_cat_t = [qc.astype(F32).T.astype(BF16) for qc in q_cat]

    def block(j, carry, fixups, span=1):
        n = range(len(chains))
        start = pl.multiple_of(j * tq, tq)
        kpos = kpos_ref[pl.ds(start, span * tq), :]
        k_cat = [jnp.concatenate([k_ref[pl.ds(start, span * tq), _head_cols(h)], kpos], axis=1) for h in heads]
        vt = [jnp.concatenate([vt_scr[h, j + b] for b in range(span)], axis=1) if span > 1 else vt_scr[h, j]
              for h in heads]
        ones = jnp.ones((SUM_ROWS, span * tq), BF16)
        vt = [jnp.concatenate([vt[h], ones], axis=0) for h in heads]
        m = carry
        s = [_dot(k_cat[h], q_cat_t[x]) for x, (h, _) in enumerate(chains)]
        if fixups is not None:
            s = [s[x] + fixups[h] for x, (h, _) in enumerate(chains)]
        m_new = [jnp.maximum(m[x], jnp.max(s[x], axis=0, keepdims=True)) for x in n]
        alpha = [jnp.exp2(m[x] - m_new[x]) for x in n]
        p = [jnp.exp2(s[x] - m_new[x]) for x in n]
        pv = [_dot(vt[h], p[x].astype(BF16)) for x, (h, _) in enumerate(chains)]
        for x in n:
            acc_scr[x] = alpha[x] * acc_scr[x] + pv[x]
        return tuple(m_new)

    acc_scr[...] = jnp.zeros_like(acc_scr)
    init = (jnp.full((1, tq), NEG_INF, F32),) * len(chains)
    carry = lax.fori_loop(0, i // span, lambda j, c: block(span * j, c, None, span), init)
    carry = lax.fori_loop(span * (i // span), i, lambda j, c: block(j, c, None), carry)
    kj = lax.broadcasted_iota(jnp.int32, (tq, tq), 0)
    qi = lax.broadcasted_iota(jnp.int32, (tq, tq), 1)
    ahead = jnp.where(kj > qi, (qi - kj).astype(F32), 0.0)
    allowed = (kj >> MASK_CHUNK_LOG2) <= (qi >> MASK_CHUNK_LOG2)
    fixups = [jnp.where(allowed, (2.0 * slope_ref[h]) * ahead, NEG_INF) for h in heads]
    block(i, carry, fixups)
    sw = sw_ref[...]
    for h in heads:
        a0, a1 = acc_scr[2 * h], acc_scr[2 * h + 1]
        l0, l1 = a0[HEAD_COLS:HEAD_COLS + 1, :], a1[HEAD_COLS:HEAD_COLS + 1, :]
        o = (a0[:HEAD_COLS] * (1.0 / l0) - (lam / l1) * a1[:HEAD_COLS]).T
        o_ref[:, _head_cols(h)] = (_rms(o, sw) * out_scale).astype(BF16)


def _diff_attn(proj, slopes2, lam, subln, kpos, qpos, *, batch, seq, out_scale, tq=256, span=2):
    smem = pl.BlockSpec(memory_space=pltpu.SMEM)
    width = DIFF_HEADS * HEAD_COLS
    kv = lambda blk: pl.BlockSpec((None, seq, width), lambda b, i: (b, 0, blk))
    return pl.pallas_call(
        functools.partial(_diff_body, tq=tq, span=span, out_scale=out_scale),
        grid=(batch, seq // tq),
        in_specs=[smem, smem,
                  pl.BlockSpec((None, tq, width), lambda b, i: (b, i, 0)),
                  kv(1), kv(2),
                  pl.BlockSpec((seq, HEAD_COLS), lambda b, i: (0, 0)),
                  pl.BlockSpec((DIFF_HEADS, 1, HEAD_COLS), lambda b, i: (0, 0, 0)),
                  pl.BlockSpec((1, HEAD_COLS), lambda b, i: (0, 0))],
        out_specs=pl.BlockSpec((None, tq, width), lambda b, i: (b, i, 0)),
        out_shape=jax.ShapeDtypeStruct((batch, seq, width), BF16),
        scratch_shapes=[pltpu.VMEM((DIFF_HEADS, seq // tq, HEAD_COLS, tq), BF16),
                        pltpu.VMEM((2 * DIFF_HEADS, HEAD_COLS + SUM_ROWS, tq), F32)],
        compiler_params=_cparams("parallel", "arbitrary"),
        name="diff_attn",
    )(slopes2, lam, proj, proj, proj, kpos, qpos, subln.reshape(1, HEAD_COLS))


def _alibi_lanes(slopes2, seq):
    pos = jnp.arange(seq)
    a = (pos >> MASK_CHUNK_LOG2).astype(F32)
    b = (pos & ((1 << MASK_CHUNK_LOG2) - 1)).astype(F32)
    kpos = jnp.zeros((seq, HEAD_COLS), F32).at[:, 0:3].set(a[:, None]).at[:, 3:6].set(b[:, None]).astype(BF16)

    def split3(x):
        x1 = x.astype(BF16).astype(F32)
        x2 = (x - x1).astype(BF16).astype(F32)
        return jnp.stack([x1, x2, (x - x1 - x2).astype(BF16).astype(F32)], axis=-1)

    mult = jnp.concatenate([split3(slopes2 * (1 << MASK_CHUNK_LOG2)), split3(slopes2)], axis=-1)
    qpos = jnp.zeros((slopes2.shape[0], 1, HEAD_COLS), F32).at[:, 0, 0:6].set(mult).astype(BF16)
    return kpos, qpos


SUM_ROWS = 16
SB_GROUP = 128


def _sb_body(q_ref, k_ref, v_ref, o_ref, vt_scr, acc_scr, *, tq, span):
    i = pl.program_id(1)
    heads = range(SB_HEADS)

    @pl.when(i == 0)
    def _():
        _store_v_transposed(v_ref, vt_scr, tq)

    q = [q_ref[:, _head_cols(h)] for h in heads]
    gs = lax.broadcasted_iota(jnp.int32, (SB_GROUP, SB_GROUP), 0)
    gj = lax.broadcasted_iota(jnp.int32, (SB_GROUP, SB_GROUP), 1)
    later = (gj > gs).astype(BF16)
    later2 = jnp.concatenate([later, later], axis=1)
    later2 = jnp.concatenate([later2, jnp.ones((SUM_ROWS, 2 * SB_GROUP), BF16)], axis=0)

    def block(j, carry, earlier, span=1):
        tail = list(carry)
        start = pl.multiple_of(j * tq, tq)
        groups = [slice(g * SB_GROUP, (g + 1) * SB_GROUP) for g in range(span * tq // SB_GROUP)]
        z = [_dot_nt(k_ref[pl.ds(start, span * tq), _head_cols(h)], q[h]) for h in heads]
        neg_keep = [jnp.maximum(z[h], 0.0) + jnp.log2(1.0 + jnp.exp2(-jnp.abs(z[h]))) for h in heads]
        log_beta = [z[h] - neg_keep[h] for h in heads]
        if earlier is not None:
            neg_keep = [jnp.where(earlier, neg_keep[h], 0.0) for h in heads]
        after = [[None] * len(groups) for _ in heads]
        for g in reversed(range(len(groups))):
            parts = [_hi_lo(neg_keep[h][groups[g], :]) for h in heads]
            within = [_dot(later2, jnp.concatenate(parts[h], axis=0)) for h in heads]
            for h in heads:
                after[h][g] = within[h][:SB_GROUP] + tail[h]
                tail[h] = tail[h] + within[h][SB_GROUP:SB_GROUP + 1]
        a = [jnp.exp2(log_beta[h] - jnp.concatenate(after[h], axis=0)) for h in heads]
        if earlier is not None:
            a = [jnp.where(earlier, a[h], 0.0) for h in heads]
        vt = [jnp.concatenate([vt_scr[h, j + b] for b in range(span)], axis=1) if span > 1 else vt_scr[h, j]
              for h in heads]
        for h in heads:
            acc_scr[h] += _dot(vt[h], a[h].astype(BF16))
        return tuple(tail)

    kj = lax.broadcasted_iota(jnp.int32, (tq, tq), 0)
    qi = lax.broadcasted_iota(jnp.int32, (tq, tq), 1)
    acc_scr[...] = jnp.zeros_like(acc_scr)
    init = (jnp.zeros((1, tq), F32),) * SB_HEADS
    carry = block(i, init, kj < qi)
    odd = i % span
    carry = lax.fori_loop(0, odd, lambda t, c: block(i - 1 - t, c, None), carry)
    lax.fori_loop(0, i // span, lambda t, c: block(i - odd - span * (t + 1), c, None, span), carry)
    for h in heads:
        o_ref[:, _head_cols(h)] = acc_scr[h].T.astype(BF16)


def _sb_attn(proj, *, batch, seq, tq=256, span=2):
    width = SB_HEADS * HEAD_COLS
    kv = lambda blk: pl.BlockSpec((None, seq, width), lambda b, i: (b, 0, blk))
    return pl.pallas_call(
        functools.partial(_sb_body, tq=tq, span=span),
        grid=(batch, seq // tq),
        in_specs=[pl.BlockSpec((None, tq, width), lambda b, i: (b, i, 3)), kv(4), kv(5)],
        out_specs=pl.BlockSpec((None, tq, width), lambda b, i: (b, i, 0)),
        out_shape=jax.ShapeDtypeStruct((batch, seq, width), BF16),
        scratch_shapes=[pltpu.VMEM((SB_HEADS, seq // tq, HEAD_COLS, tq), BF16),
                        pltpu.VMEM((SB_HEADS, HEAD_COLS, tq), F32)],
        compiler_params=_cparams("parallel", "arbitrary"),
        name="sb_attn",
    )(proj, proj, proj)


def _gdn_in_body(x_ref, xp_ref, nw_ref, wqkv_ref, wg_ref, wba_ref, cw_ref, alog_ref, dt_ref,
                 q_ref, k_ref, v_ref, gate_ref, bg_ref, p_scr, *, tm, seq, n_chunk):
    i = pl.program_id(0)
    nw = nw_ref[...]
    h = _rms(x_ref[...], nw).astype(BF16)
    seq_start = (i * tm) % seq == 0
    hp = jnp.where(seq_start, 0.0, _rms(xp_ref[...], nw)).astype(BF16)
    mix = GDN_HEADS * HEAD_COLS
    outs = (q_ref, k_ref, v_ref)
    q_scale = HEAD_COLS ** -0.5
    for c in range(0, 3 * mix, n_chunk):
        p_scr[0:CONV_HALO, :] = _dot(hp, wqkv_ref[:, c:c + n_chunk])
        p_scr[CONV_HALO:CONV_HALO + tm, :] = _dot(h, wqkv_ref[:, c:c + n_chunk])
        for s in range(0, n_chunk, HEAD_COLS):
            col = c + s
            win = p_scr[CONV_HALO - CONV_LEAD:CONV_HALO + tm, s:s + HEAD_COLS]
            y = None
            for tap in range(GDN_CONV):
                delay = GDN_CONV - 1 - tap
                shifted = pltpu.roll(win, delay, axis=0) if delay else win
                term = shifted[CONV_LEAD:, :] * cw_ref[tap:tap + 1, col:col + HEAD_COLS]
                y = term if y is None else y + term
            y = _silu(y)
            which, local = divmod(col, mix)
            if which < 2:
                y = y * lax.rsqrt(jnp.sum(y * y, axis=-1, keepdims=True) + L2_EPS)
            if which == 0:
                y = y * q_scale
            outs[which][:, local:local + HEAD_COLS] = y.astype(BF16)
    for c in range(0, mix, n_chunk):
        gate_ref[:, c:c + n_chunk] = _dot(h, wg_ref[:, c:c + n_chunk]).astype(BF16)
    ba = _dot(h, wba_ref[...])
    lane = lax.broadcasted_iota(jnp.int32, ba.shape, 1)
    beta = jax.nn.sigmoid(ba)
    g = -jnp.exp(alog_ref[...]) * _softplus(ba + dt_ref[...])
    bg_ref[...] = jnp.where(lane < GDN_HEADS, beta, jnp.where(lane < 2 * GDN_HEADS, g, 0.0))


def _gdn_in(x, nw, w_in, conv_w, a_log, dt_bias, *, seq, tm=512, n_chunk=512):
    m, d = x.shape
    mix = GDN_HEADS * HEAD_COLS
    w_qkv = w_in[:, :3 * mix].astype(BF16)
    w_gate = w_in[:, 3 * mix:4 * mix].astype(BF16)
    pad = HEAD_COLS - 2 * GDN_HEADS
    w_ba = jnp.pad(w_in[:, 4 * mix:], ((0, 0), (0, pad))).astype(BF16)
    lanes = lambda v: jnp.pad(v.astype(F32), (GDN_HEADS, pad)).reshape(1, HEAD_COLS)
    row = lambda n: pl.BlockSpec((tm, n), lambda i: (i, 0))
    halo_blocks = tm // CONV_HALO
    act = jax.ShapeDtypeStruct((m, mix), BF16)
    return pl.pallas_call(
        functools.partial(_gdn_in_body, tm=tm, seq=seq, n_chunk=n_chunk),
        grid=(m // tm,),
        in_specs=[row(d),
                  pl.BlockSpec((CONV_HALO, d), lambda i: (jnp.maximum(i * halo_blocks - 1, 0), 0)),
                  _resident((1, d)), _resident((d, 3 * mix)), _resident((d, mix)), _resident((d, HEAD_COLS)),
                  _resident((GDN_CONV, 3 * mix)), _resident((1, HEAD_COLS)), _resident((1, HEAD_COLS))],
        out_specs=[row(mix), row(mix), row(mix), row(mix), row(HEAD_COLS)],
        out_shape=[act, act, act, act, jax.ShapeDtypeStruct((m, HEAD_COLS), F32)],
        scratch_shapes=[pltpu.VMEM((CONV_HALO + tm, n_chunk), F32)],
        compiler_params=_cparams("parallel"),
        name="gdn_in",
    )(x, x, nw.reshape(1, d), w_qkv, w_gate, w_ba, conv_w.astype(F32), lanes(a_log), lanes(dt_bias))


def _split3(x):
    b1 = x.astype(BF16)
    r1 = x - b1.astype(F32)
    b2 = r1.astype(BF16)
    b3 = (r1 - b2.astype(F32)).astype(BF16)
    return b1, b2, b3


def _hi_lo(x):
    hi = x.astype(BF16)
    return hi, (x - hi.astype(F32)).astype(BF16)


def _dot_split(a_parts, b_parts):
    ah, al = a_parts
    bh, bl = b_parts
    c = bh.shape[0]
    rhs = jnp.concatenate([jnp.concatenate([bh, bl], axis=1),
                           jnp.concatenate([bh, jnp.zeros_like(bh)], axis=1)], axis=0)
    r = _dot(jnp.concatenate([ah, al], axis=1), rhs)
    return r[:, :c] + r[:, c:]


def _unit_lower_inverse(lows, eye):
    c = lows[0].shape[0]
    n = range(len(lows))
    row = lax.broadcasted_iota(jnp.int32, (c, c), 0)
    col = lax.broadcasted_iota(jnp.int32, (c, c), 1)
    t_inv = None
    for level in range(int(math.log2(c))):
        lower_left = ((row >> (level + 1)) == (col >> (level + 1))) & (((row >> level) & 1) == 1) \
            & (((col >> level) & 1) == 0)
        off = [jnp.where(lower_left, low, 0.0) for low in lows]
        if t_inv is None:
            t_inv = [eye - off[i] for i in n]
            continue
        x = [_dot_split(_hi_lo(off[i]), _hi_lo(t_inv[i])) for i in n]
        t_inv = [t_inv[i] - _dot_split(_hi_lo(t_inv[i]), _hi_lo(x[i])) for i in n]
    return t_inv


def _gdn_body(q_ref, k_ref, v_ref, gate_ref, bg_ref, nw_ref, o_ref, s_scr):
    c = GDN_CHUNK

    @pl.when(pl.program_id(1) == 0)
    def _():
        s_scr[...] = jnp.zeros_like(s_scr)

    row = lax.broadcasted_iota(jnp.int32, (c, c), 0)
    col = lax.broadcasted_iota(jnp.int32, (c, c), 1)
    incl = col <= row
    strict = col < row
    eye = (col == row).astype(F32)
    tri = incl.astype(BF16)
    bg = bg_ref[...]
    gc_all = sum(_dot(tri, part) for part in _split3(bg))
    gc_rows = gc_all.T
    nw = nw_ref[...]
    heads = range(GDN_HEADS)
    sl = [slice(h * HEAD_COLS, (h + 1) * HEAD_COLS) for h in heads]
    state = [s_scr[h] for h in heads]
    q = [q_ref[:, sl[h]].astype(F32) for h in heads]
    k = [k_ref[:, sl[h]].astype(F32) for h in heads]
    v = [v_ref[:, sl[h]].astype(F32) for h in heads]
    beta = [bg[:, h:h + 1] for h in heads]
    gcb = [jnp.broadcast_to(gc_all[:, GDN_HEADS + h:GDN_HEADS + h + 1], (c, c)) for h in heads]
    gct = [jnp.broadcast_to(gc_rows[GDN_HEADS + h:GDN_HEADS + h + 1, :], (c, c)) for h in heads]
    decay = [jnp.exp(jnp.where(incl, gcb[h] - gct[h], NEG_INF)) for h in heads]
    eg = [jnp.exp(gcb[h]) for h in heads]
    g_last = [gcb[h][c - 1:c, :] for h in heads]
    kb = [k[h] * beta[h] for h in heads]
    kbf = [k[h].astype(BF16) for h in heads]
    low = [jnp.where(strict, _dot_nt(kb[h].astype(BF16), kbf[h]) * decay[h], 0.0) for h in heads]
    aqk = [jnp.where(incl, _dot_nt(q[h].astype(BF16), kbf[h]) * decay[h], 0.0).astype(BF16) for h in heads]
    tb = [t.astype(BF16) for t in _unit_lower_inverse(low, eye)]
    uw = [_dot(tb[h], jnp.concatenate([(v[h] * beta[h]).astype(BF16), (kb[h] * eg[h]).astype(BF16)], axis=1))
          for h in heads]
    sb = [state[h].astype(BF16) for h in heads]
    vnb = [(uw[h][:, :HEAD_COLS] - _dot(uw[h][:, HEAD_COLS:].astype(BF16), sb[h])).astype(BF16) for h in heads]
    o = [_dot(jnp.concatenate([(q[h] * eg[h]).astype(BF16), aqk[h]], axis=1),
              jnp.concatenate([sb[h], vnb[h]], axis=0)) for h in heads]
    k_dec = [(k[h] * jnp.exp(g_last[h] - gcb[h])).T.astype(BF16) for h in heads]
    new_state = [state[h] * jnp.exp(g_last[h]) + _dot(k_dec[h], vnb[h]) for h in heads]
    for h in heads:
        s_scr[h] = new_state[h]
        o_ref[:, sl[h]] = (_rms(o[h], nw) * _silu(gate_ref[:, sl[h]].astype(F32))).astype(BF16)


def _gdn(q, k, v, gate, bg, norm_w, *, batch, seq):
    mix = GDN_HEADS * HEAD_COLS
    c = GDN_CHUNK
    per_seq = seq // c
    row = lambda n: pl.BlockSpec((c, n), lambda b, t: (b * per_seq + t, 0))
    return pl.pallas_call(
        _gdn_body,
        grid=(batch, per_seq),
        in_specs=[row(mix), row(mix), row(mix), row(mix), row(HEAD_COLS),
                  pl.BlockSpec((1, HEAD_COLS), lambda b, t: (0, 0))],
        out_specs=row(mix),
        out_shape=jax.ShapeDtypeStruct((batch * seq, mix), BF16),
        scratch_shapes=[pltpu.VMEM((GDN_HEADS, HEAD_COLS, HEAD_COLS), F32)],
        compiler_params=_cparams("parallel", "arbitrary"),
        name="gdn",
    )(q, k, v, gate, bg, norm_w.reshape(1, HEAD_COLS))


def _even_mixer(x, norm_w, w_in, diff_lambda, diff_subln, w_out, layer_idx, *, batch, seq):
    n_in = w_in.shape[1]
    q_cols = DIFF_HEADS * HEAD_COLS
    sb_q0 = 3 * q_cols
    col = jnp.arange(n_in)
    log2e = math.log2(math.e)
    col_scale = jnp.where(col < q_cols, DIFF_QK_DIM ** -0.5 * log2e,
                          jnp.where((col >= sb_q0) & (col < sb_q0 + SB_HEADS * HEAD_COLS),
                                    HEAD_COLS ** -0.5 * log2e, 1.0))
    proj = _att_in(x, norm_w, w_in, col_scale.astype(F32).reshape(1, n_in)).reshape(batch, seq, n_in)
    lambda_init = 0.8 - 0.6 * math.exp(-0.3 * layer_idx)
    lp = diff_lambda.astype(F32)
    lam = jnp.exp(jnp.sum(lp[0] * lp[1])) - jnp.exp(jnp.sum(lp[2] * lp[3])) + lambda_init
    slopes2 = 2.0 ** (-8.0 * jnp.arange(1, DIFF_HEADS + 1, dtype=F32) / DIFF_HEADS) * log2e
    kpos, qpos = _alibi_lanes(slopes2, seq)
    oa = _diff_attn(proj, slopes2, lam.reshape(1), diff_subln.astype(F32), kpos, qpos, batch=batch, seq=seq,
                    out_scale=1.0 - lambda_init)
    osb = _sb_attn(proj, batch=batch, seq=seq)
    m = batch * seq
    return (oa.reshape(m, -1), osb.reshape(m, -1)), (w_out[:q_cols], w_out[q_cols:])


def _odd_mixer(x, norm_w, w_in, conv_w, a_log, dt_bias, gdn_norm, w_out, *, batch, seq):
    q, k, v, gate, bg = _gdn_in(x, norm_w, w_in, conv_w, a_log, dt_bias, seq=seq)
    o = _gdn(q, k, v, gate, bg, gdn_norm.astype(F32), batch=batch, seq=seq)
    return (o,), (w_out,)


def kernel(x, ffn1_norm, ffn1_w_gate, ffn1_w_up, ffn1_w_down, mix_norm, att_w_in, diff_lambda, diff_subln, att_w_out, gdn_w_in, gdn_conv_w, gdn_a_log, gdn_dt_bias, gdn_norm, gdn_w_out, ffn2_norm, ffn2_w_gate, ffn2_w_up, ffn2_w_down, final_norm):
    batch, seq, d = x.shape
    depth = ffn1_norm.shape[0]
    xf = x.reshape(batch * seq, d)
    for layer in range(depth):
        xf = _ffn(xf, ffn1_norm[layer], ffn1_w_gate[layer], ffn1_w_up[layer], ffn1_w_down[layer])
        if layer % 2 == 0:
            e = layer // 2
            mix, w_out = _even_mixer(xf, mix_norm[layer], att_w_in[e], diff_lambda[e], diff_subln[e], att_w_out[e],
                                     layer, batch=batch, seq=seq)
        else:
            o = layer // 2
            mix, w_out = _odd_mixer(xf, mix_norm[layer], gdn_w_in[o], gdn_conv_w[o], gdn_a_log[o], gdn_dt_bias[o],
                                    gdn_norm[o], gdn_w_out[o], batch=batch, seq=seq)
        last = layer == depth - 1
        xf = _ffn(xf, ffn2_norm[layer], ffn2_w_gate[layer], ffn2_w_up[layer], ffn2_w_down[layer],
                  final_w=final_norm if last else None, mix=mix, w_out=w_out)
    return xf.reshape(batch, seq, d)
```
